```python
import jax, jax.numpy as jnp
from jax import lax
import numpy as np

D_MODEL = 1024
BATCH = 2
SEQ = 8192
DEPTH = 1

N_META = 16
RG_WIDTH = (4 * D_MODEL // 3) // 128 * 128
RG_BLOCKS = 16
RG_BLOCK = RG_WIDTH // RG_BLOCKS
RG_CONV = 4
RG_C = 8.0
HG_HEADS = 8
HG_DK = D_MODEL // HG_HEADS
HG_DV = D_MODEL // HG_HEADS
HG_WIDTH = HG_HEADS * HG_DV
HG_CHUNK = 64
D_FF = ((8 * D_MODEL // 3 + 255) // 256) * 256
NORM_EPS = 1e-6
SPLITS = [RG_WIDTH, RG_WIDTH, HG_WIDTH, HG_WIDTH, HG_WIDTH, HG_WIDTH, D_MODEL, D_MODEL]
D_IN = sum(SPLITS)

kernel_name = "hybrid_rglru_hgrn2_gated_block"


def _rmsnorm(x, g):
    xf = x.astype(jnp.float32)
    y = xf * lax.rsqrt(jnp.mean(xf * xf, axis=-1, keepdims=True) + NORM_EPS)
    return (y * g.astype(jnp.float32)).astype(x.dtype)


def _causal_depthwise_conv(x, w, b):
    k = w.shape[0]
    y = lax.conv_general_dilated(
        x, w[:, None, :], window_strides=(1,), padding=[(k - 1, 0)],
        dimension_numbers=("NWC", "WIO", "NWC"), feature_group_count=x.shape[-1])
    return y + b


def _rg_lru(x, w_a, b_a, w_x, b_x, lam):
    bsz, t, w = x.shape
    xb = x.reshape(bsz, t, RG_BLOCKS, RG_BLOCK)
    r = jax.nn.sigmoid(jnp.einsum("btnd,nde->btne", xb, w_a).reshape(bsz, t, w) + b_a)
    i = jax.nn.sigmoid(jnp.einsum("btnd,nde->btne", xb, w_x).reshape(bsz, t, w) + b_x)
    log_a = -RG_C * r.astype(jnp.float32) * jax.nn.softplus(-lam.astype(jnp.float32))
    a = jnp.exp(log_a)
    u = jnp.sqrt(-jnp.expm1(2.0 * log_a)) * (i * x).astype(jnp.float32)

    def combine(left, right):
        a1, b1 = left
        a2, b2 = right
        return a1 * a2, a2 * b1 + b2

    _, h = lax.associative_scan(combine, (a, u), axis=1)
    return h.astype(x.dtype)


def _hgrn2_chunkwise(q, k, v, log_f):
    bsz, nh, t, dk = q.shape
    dv = v.shape[-1]
    n = t // HG_CHUNK

    def to_chunks(z):
        z = z.astype(jnp.float32).reshape(bsz, nh, n, HG_CHUNK, z.shape[-1])
        return jnp.moveaxis(z, 2, 0)

    causal = jnp.tril(jnp.ones((HG_CHUNK, HG_CHUNK), dtype=bool))

    def step(s_prev, inp):
        qc, kc, vc, gc = inp
        bcum = jnp.cumsum(gc, axis=-2)
        o_inter = jnp.einsum("bhtk,bhkv->bhtv", qc * jnp.exp(bcum), s_prev)
        diff = bcum[:, :, :, None, :] - bcum[:, :, None, :, :]
        decay = jnp.exp(jnp.where(causal[:, :, None], diff, -jnp.inf))
        attn = jnp.einsum("bhtk,bhtsk,bhsk->bhts", qc, decay, kc)
        o_intra = jnp.einsum("bhts,bhsv->bhtv", attn, vc)
        b_last = bcum[:, :, -1:, :]
        k_dec = kc * jnp.exp(b_last - bcum)
        s_new = jnp.exp(b_last[:, :, 0, :])[..., None] * s_prev + jnp.einsum("bhsk,bhsv->bhkv", k_dec, vc)
        return s_new, o_inter + o_intra

    s0 = jnp.zeros((bsz, nh, dk, dv), jnp.float32)
    _, o = lax.scan(step, s0, (to_chunks(q), to_chunks(k), to_chunks(v), to_chunks(log_f)))
    o = jnp.moveaxis(o, 0, 2).reshape(bsz, nh, t, dv)
    return o


def setup_inputs(seed: int = 0) -> dict:
    key = jax.random.key(seed)
    ks = jax.random.split(key, 24)
    f32 = jnp.float32

    def nrm(k, shape, scale):
        return jax.random.normal(k, shape, f32) * scale

    a_c = jax.random.uniform(ks[10], (DEPTH, RG_WIDTH), f32, 0.9, 0.999)
    p = a_c ** (1.0 / RG_C)
    rg_lambda = jnp.log(p) - jnp.log1p(-p)
    return {
        "x": nrm(ks[0], (BATCH, SEQ, D_MODEL), 1.0),
        "meta_tokens": nrm(ks[1], (N_META, D_MODEL), 1.0),
        "norm1_g": 1.0 + nrm(ks[2], (DEPTH, D_MODEL), 0.01),
        "w_in": nrm(ks[3], (DEPTH, D_MODEL, D_IN), D_MODEL ** -0.5),
        "conv_w": nrm(ks[4], (DEPTH, RG_CONV, RG_WIDTH), RG_CONV ** -0.5),
        "conv_b": nrm(ks[5], (DEPTH, RG_WIDTH), 0.01),
        "rg_wa": nrm(ks[6], (DEPTH, RG_BLOCKS, RG_BLOCK, RG_BLOCK), RG_BLOCK ** -0.5),
        "rg_ba": nrm(ks[7], (DEPTH, RG_WIDTH), 0.01),
        "rg_wx": nrm(ks[8], (DEPTH, RG_BLOCKS, RG_BLOCK, RG_BLOCK), RG_BLOCK ** -0.5),
        "rg_bx": nrm(ks[9], (DEPTH, RG_WIDTH), 0.01),
        "rg_lambda": rg_lambda,
        "hg_lb_logits": nrm(ks[11], (DEPTH + 1, HG_HEADS * HG_DK), 1.0),
        "hg_norm_g": 1.0 + nrm(ks[12], (DEPTH, HG_DV), 0.01),
        "w_proj_a": nrm(ks[13], (DEPTH, RG_WIDTH, D_MODEL), RG_WIDTH ** -0.5),
        "w_proj_b": nrm(ks[14], (DEPTH, HG_WIDTH, D_MODEL), HG_WIDTH ** -0.5),
        "w_out": nrm(ks[15], (DEPTH, D_MODEL, D_MODEL), D_MODEL ** -0.5),
        "norm2_g": 1.0 + nrm(ks[16], (DEPTH, D_MODEL), 0.01),
        "w_ffn_in": nrm(ks[17], (DEPTH, D_MODEL, 2 * D_FF), D_MODEL ** -0.5),
        "w_ffn_down": nrm(ks[18], (DEPTH, D_FF, D_MODEL), D_FF ** -0.5),
        "norm_f_g": 1.0 + nrm(ks[19], (D_MODEL,), 0.01),
    }


def reference(x, meta_tokens, norm1_g, w_in, conv_w, conv_b, rg_wa, rg_ba, rg_wx, rg_bx,
              rg_lambda, hg_lb_logits, hg_norm_g, w_proj_a, w_proj_b, w_out, norm2_g,
              w_ffn_in, w_ffn_down, norm_f_g):
    bsz = x.shape[0]
    meta = jnp.broadcast_to(meta_tokens[None].astype(x.dtype), (bsz, N_META, D_MODEL))
    h = jnp.concatenate([meta, x], axis=1)
    t_total = h.shape[1]
    pad = HG_CHUNK - N_META

    lower_bounds = jnp.cumsum(jax.nn.softmax(hg_lb_logits.astype(jnp.float32), axis=0), axis=0)
    split_idx = list(np.cumsum(SPLITS)[:-1])

    for l in range(DEPTH):
        hn = _rmsnorm(h, norm1_g[l])
        proj = jnp.einsum("btd,de->bte", hn, w_in[l])
        a_x, a_gate, b_q, b_f, b_i, b_g, gate_a, gate_b = jnp.split(proj, split_idx, axis=-1)

        a_x = _causal_depthwise_conv(a_x, conv_w[l], conv_b[l])
        y_a = _rg_lru(a_x, rg_wa[l], rg_ba[l], rg_wx[l], rg_bx[l], rg_lambda[l]) * jax.nn.gelu(a_gate)

        lb = lower_bounds[l].astype(jnp.float32)
        f_logit = b_f.astype(jnp.float32)
        log_f = jnp.log(lb + (1.0 - lb) * jax.nn.sigmoid(f_logit))
        k_in = (1.0 - lb) * jax.nn.sigmoid(-f_logit)
        q_in = jax.nn.silu(b_q)

        def heads(z, d):
            z = z.reshape(bsz, t_total, HG_HEADS, d).transpose(0, 2, 1, 3)
            return jnp.pad(z, ((0, 0), (0, 0), (pad, 0), (0, 0)))

        o_b = _hgrn2_chunkwise(heads(q_in, HG_DK), heads(k_in, HG_DK),
                               heads(b_i, HG_DV), heads(log_f, HG_DK))
        o_b = o_b[:, :, pad:, :].transpose(0, 2, 1, 3)
        g_heads = b_g.reshape(bsz, t_total, HG_HEADS, HG_DV)
        y_b = (_rmsnorm(o_b, hg_norm_g[l]).astype(h.dtype) * jax.nn.silu(g_heads)).reshape(bsz, t_total, HG_WIDTH)

        merged = (jax.nn.sigmoid(gate_a) * jnp.einsum("btw,wd->btd", y_a, w_proj_a[l])
                  + jax.nn.sigmoid(gate_b) * jnp.einsum("btw,wd->btd", y_b, w_proj_b[l]))
        h = h + jnp.einsum("btd,de->bte", merged, w_out[l])

        hn = _rmsnorm(h, norm2_g[l])
        gu = jnp.einsum("btd,df->btf", hn, w_ffn_in[l])
        g_ff, u_ff = jnp.split(gu, [D_FF], axis=-1)
        h = h + jnp.einsum("btf,fd->btd", jax.nn.silu(g_ff) * u_ff, w_ffn_down[l])

    h = _rmsnorm(h, norm_f_g)
    return h[:, N_META:, :]
```

```python
import functools
import math

import jax
import jax.numpy as jnp
from jax import lax
from jax.experimental import pallas as pl
from jax.experimental.pallas import tpu as pltpu

D_MODEL = 1024
N_META = 16
RG_WIDTH = 1280
RG_BLOCKS = 16
RG_BLOCK = 80
RG_GROUPS = 2
RG_GROUP_W = RG_WIDTH // RG_GROUPS
RG_CONV = 4
RG_C = 8.0
HG_HEADS = 8
HG_DK = 128
HG_WIDTH = 1024
D_FF = 2816
NORM_EPS = 1e-6

OFF_AX, OFF_AG, OFF_Q, OFF_F, OFF_I, OFF_G, OFF_GA, OFF_GB = (
    0, 1280, 2560, 3584, 4608, 5632, 6656, 7680)

SUBLANES = 8
TIME_TILE = 128
CHUNK = 64
FFN_ROWS = 512
FFN_COLS = 1408
VMEM_LIMIT_BYTES = 60000 * 1024

_BF = jnp.bfloat16
_F32 = jnp.float32


def _dot(a, b):
    return jnp.dot(a, b, preferred_element_type=_F32)


def _dot_nt(a, b):
    return lax.dot_general(a, b, (((1,), (1,)), ((), ())), preferred_element_type=_F32)


def _dot_tn(a, b):
    return lax.dot_general(a, b, (((0,), (0,)), ((), ())), preferred_element_type=_F32)


def _rmsnorm(x, g):
    ms = jnp.mean(x * x, axis=-1, keepdims=True)
    return x * lax.rsqrt(ms + NORM_EPS) * g


def _sigmoid(x):
    return jax.nn.sigmoid(x)


def _silu(x):
    return x * _sigmoid(x)


def _gelu_tanh(x):
    c = math.sqrt(2.0 / math.pi)
    return x * (0.5 * (1.0 + jnp.tanh(c * (x + 0.044715 * (x * x * x)))))


def _bcast_rows(row, n):
    return jnp.broadcast_to(row, (n, row.shape[-1]))


def _level_masks():
    t = lax.broadcasted_iota(jnp.int32, (CHUNK, CHUNK), 0)
    s = lax.broadcasted_iota(jnp.int32, (CHUNK, CHUNK), 1)
    masks = {0: t == s}
    h = 1
    while h < CHUNK:
        sh = int(math.log2(2 * h))
        masks[h] = ((t >> sh) == (s >> sh)) & ((t & h) != 0) & ((s & h) == 0)
        h *= 2
    return masks


def _hgrn_chunk(g, q, k, v_bf, st_ref, b, masks):
    w = g.shape[-1]
    nv = CHUNK // SUBLANES
    sub = lax.broadcasted_iota(jnp.int32, (SUBLANES, w), 0)
    bcs = []
    run = None
    for j in range(nv):
        x = g[SUBLANES * j:SUBLANES * (j + 1)]
        for s in (1, 2, 4):
            x = x + jnp.where(sub >= s, pltpu.roll(x, s, 0), 0.0)
        if run is not None:
            x = x + run
        bcs.append(x)
        run = x[SUBLANES - 1:SUBLANES]
    bc = jnp.concatenate(bcs, axis=0)
    b_last = run

    bm = {}
    h = CHUNK // 2
    while h >= SUBLANES:
        nblk = CHUNK // (2 * h)
        parts = []
        for blk in range(nblk):
            r = blk * 2 * h + h - 1
            parts.append(_bcast_rows(bcs[r // SUBLANES][SUBLANES - 1:SUBLANES], 2 * h))
        bm[h] = parts[0] if nblk == 1 else jnp.concatenate(parts, axis=0)
        h //= 2
    bm[4] = jnp.concatenate([_bcast_rows(x[3:4], SUBLANES) for x in bcs], axis=0)
    bm[2] = jnp.concatenate(
        [jnp.where(sub < 4, _bcast_rows(x[1:2], SUBLANES), _bcast_rows(x[5:6], SUBLANES))
         for x in bcs], axis=0)
    bm[1] = jnp.concatenate(
        [jnp.where(sub < 2, _bcast_rows(x[0:1], SUBLANES),
                   jnp.where(sub < 4, _bcast_rows(x[2:3], SUBLANES),
                             jnp.where(sub < 6, _bcast_rows(x[4:5], SUBLANES),
                                       _bcast_rows(x[6:7], SUBLANES))))
         for x in bcs], axis=0)

    levels = sorted(bm.keys(), reverse=True)
    q_bf = q.astype(_BF)
    k_bf = k.astype(_BF)
    ql = {h: (q * jnp.exp(jnp.minimum(bc - bm[h], 0.0))).astype(_BF) for h in levels}
    kl = {h: (k * jnp.exp(jnp.minimum(bm[h] - bc, 0.0))).astype(_BF) for h in levels}
    q_dec = (q * jnp.exp(bc)).astype(_BF)
    k_dec = (k * jnp.exp(b_last - bc)).astype(_BF)
    s_dec = jnp.exp(b_last)

    outs = []
    for hd in range(HG_HEADS):
        sl = slice(hd * HG_DK, (hd + 1) * HG_DK)
        st = st_ref[b, hd]
        scores = jnp.where(masks[0], _dot_nt(q_bf[:, sl], k_bf[:, sl]), 0.0)
        for h in levels:
            scores = jnp.where(masks[h], _dot_nt(ql[h][:, sl], kl[h][:, sl]), scores)
        o = _dot(scores.astype(_BF), v_bf[:, sl])
        o = o + _dot_nt(q_dec[:, sl], st.astype(_BF))
        st_ref[b, hd] = s_dec[:, sl] * st + _dot_tn(v_bf[:, sl], k_dec[:, sl])
        outs.append(o)
    return jnp.concatenate(outs, axis=-1)


def _mixer_kernel(meta_ref, x_ref, n1g_ref, w_in_ref, convw_ref, convb_ref, wg_ref,
                  ba_ref, bx_ref, lam_ref, lbl_ref, hgn_ref, wpa_ref, wpb_ref, wout_ref,
                  o_ref, cbuf, hcar, st_ref, *, nb, tt):
    i = pl.program_id(0)
    rows = nb * tt
    is0 = i == 0

    @pl.when(is0)
    def _():
        cbuf[...] = jnp.zeros_like(cbuf)
        hcar[...] = jnp.zeros_like(hcar)
        st_ref[...] = jnp.zeros_like(st_ref)

    meta = meta_ref[...]
    h_in = jnp.concatenate([jnp.where(is0, meta, x_ref[b]) for b in range(nb)], axis=0)
    hn = _rmsnorm(h_in, n1g_ref[...]).astype(_BF)

    def proj(off, width):
        return _dot(hn, w_in_ref[:, off:off + width])

    a_x = proj(OFF_AX, RG_WIDTH)
    convw = convw_ref[...]
    xc_parts = []
    for b in range(nb):
        cbuf[b, SUBLANES:, :] = a_x[b * tt:(b + 1) * tt]
        acc = convb_ref[...] + convw[RG_CONV - 1:RG_CONV] * cbuf[b, pl.ds(SUBLANES, tt), :]
        for j in range(RG_CONV - 1):
            off = SUBLANES - (RG_CONV - 1) + j
            acc = acc + convw[j:j + 1] * cbuf[b, pl.ds(off, tt), :]
        xc_parts.append(acc)
        cbuf[b, 0:SUBLANES, :] = cbuf[b, pl.ds(tt, SUBLANES), :]
    xc = jnp.concatenate(xc_parts, axis=0)
    xc_bf = xc.astype(_BF)
    r_parts, i_parts = [], []
    for gidx in range(RG_GROUPS):
        lo = gidx * RG_GROUP_W
        z = _dot(xc_bf[:, lo:lo + RG_GROUP_W], wg_ref[gidx])
        r_parts.append(_sigmoid(z[:, :RG_GROUP_W] + ba_ref[:, lo:lo + RG_GROUP_W]))
        i_parts.append(_sigmoid(z[:, RG_GROUP_W:] + bx_ref[:, lo:lo + RG_GROUP_W]))
    r_gate = jnp.concatenate(r_parts, axis=-1)
    i_gate = jnp.concatenate(i_parts, axis=-1)
    z = -lam_ref[...]
    softplus = jnp.maximum(z, 0.0) + jnp.log1p(jnp.exp(-jnp.abs(z)))
    log_a = (-RG_C) * r_gate * softplus
    a = jnp.exp(log_a)
    u = jnp.sqrt(1.0 - jnp.exp(2.0 * log_a)) * (i_gate * xc)
    t_loc = lax.broadcasted_iota(jnp.int32, (rows, 1), 0) % tt
    u = jnp.where(jnp.logical_or(i > 0, t_loc >= tt - N_META), u, 0.0)

    a_gate = proj(OFF_AG, RG_WIDTH)
    gate_act = _gelu_tanh(a_gate)

    def scan_scope(abuf, ubuf):
        abuf[...] = a
        ubuf[...] = u
        sub = lax.broadcasted_iota(jnp.int32, (SUBLANES, RG_WIDTH), 0)

        def body(j, carry):
            new = []
            for b in range(nb):
                r0 = pl.multiple_of(b * tt + j * SUBLANES, SUBLANES)
                av = abuf[pl.ds(r0, SUBLANES), :]
                hv = ubuf[pl.ds(r0, SUBLANES), :]
                for s in (1, 2, 4):
                    keep = sub >= s
                    hv = hv + av * jnp.where(keep, pltpu.roll(hv, s, 0), 0.0)
                    av = av * jnp.where(keep, pltpu.roll(av, s, 0), 1.0)
                hv = hv + av * carry[b]
                ubuf[pl.ds(r0, SUBLANES), :] = hv
                new.append(hv[SUBLANES - 1:SUBLANES])
            return tuple(new)

        init = tuple(hcar[b] for b in range(nb))
        fin = lax.fori_loop(0, tt // SUBLANES, body, init)
        for b in range(nb):
            hcar[b] = fin[b]
        return ubuf[...]

    h_rg = pl.run_scoped(scan_scope,
                         pltpu.VMEM((rows, RG_WIDTH), _F32),
                         pltpu.VMEM((rows, RG_WIDTH), _F32))
    y_a = (h_rg * gate_act).astype(_BF)
    merged = _sigmoid(proj(OFF_GA, D_MODEL)) * _dot(y_a, wpa_ref[...])

    l0 = lbl_ref[0:1, :]
    l1 = lbl_ref[1:2, :]
    lmax = jnp.maximum(l0, l1)
    e0 = jnp.exp(l0 - lmax)
    e1 = jnp.exp(l1 - lmax)
    lb = e0 / (e0 + e1)
    f_logit = proj(OFF_F, HG_WIDTH)
    log_f = jnp.log(lb + (1.0 - lb) * _sigmoid(f_logit))
    k_in = (1.0 - lb) * _sigmoid(-f_logit)
    q_in = _silu(proj(OFF_Q, HG_WIDTH))
    v_in = proj(OFF_I, HG_WIDTH).astype(_BF)

    def hgrn_scope(gbuf, qbuf, kbuf, vbuf):
        gbuf[...] = log_f
        qbuf[...] = q_in
        kbuf[...] = k_in
        vbuf[...] = v_in
        masks = _level_masks()

        def body(c, _):
            for b in range(nb):
                r0 = pl.multiple_of(b * tt + c * CHUNK, CHUNK)
                o = _hgrn_chunk(gbuf[pl.ds(r0, CHUNK), :], qbuf[pl.ds(r0, CHUNK), :],
                                kbuf[pl.ds(r0, CHUNK), :], vbuf[pl.ds(r0, CHUNK), :],
                                st_ref, b, masks)
                qbuf[pl.ds(r0, CHUNK), :] = o
            return 0

        lax.fori_loop(0, tt // CHUNK, body, 0)
        return qbuf[...]

    o_b = pl.run_scoped(hgrn_scope,
                        pltpu.VMEM((rows, HG_WIDTH), _F32),
                        pltpu.VMEM((rows, HG_WIDTH), _F32),
                        pltpu.VMEM((rows, HG_WIDTH), _F32),
                        pltpu.VMEM((rows, HG_WIDTH), _BF))
    g_act = _silu(proj(OFF_G, HG_WIDTH))
    hgn = hgn_ref[...]
    yb_parts = []
    for hd in range(HG_HEADS):
        sl = slice(hd * HG_DK, (hd + 1) * HG_DK)
        yb_parts.append((_rmsnorm(o_b[:, sl], hgn) * g_act[:, sl]).astype(_BF))
    y_b = jnp.concatenate(yb_parts, axis=-1)
    merged = merged + _sigmoid(proj(OFF_GB, D_MODEL)) * _dot(y_b, wpb_ref[...])

    out = h_in + _dot(merged.astype(_BF), wout_ref[...])
    for b in range(nb):
        o_ref[b] = out[b * tt:(b + 1) * tt]


def _ffn_kernel(h_ref, n2g_ref, wfi_ref, wfd_ref, nfg_ref, o_ref):
    h = h_ref[...]
    hn = _rmsnorm(h, n2g_ref[...]).astype(_BF)
    acc = h
    for j in range(D_FF // FFN_COLS):
        lo = j * FFN_COLS
        gcol = _dot(hn, wfi_ref[:, lo:lo + FFN_COLS])
        ucol = _dot(hn, wfi_ref[:, D_FF + lo:D_FF + lo + FFN_COLS])
        act = (_silu(gcol) * ucol).astype(_BF)
        acc = acc + _dot(act, wfd_ref[lo:lo + FFN_COLS, :])
    o_ref[...] = _rmsnorm(acc, nfg_ref[...])


def _resident(shape):
    nd = len(shape)
    return pl.BlockSpec(shape, lambda *_: (0,) * nd, pipeline_mode=pl.Buffered(1))


def _small(shape):
    nd = len(shape)
    return pl.BlockSpec(shape, lambda *_: (0,) * nd)


def _block_diag_groups(w):
    per = RG_BLOCKS // RG_GROUPS
    w = w.reshape(RG_GROUPS, per, RG_BLOCK, RG_BLOCK)
    eye = jnp.eye(per, dtype=w.dtype)
    bd = w[:, :, :, None, :] * eye[None, :, None, :, None]
    return bd.reshape(RG_GROUPS, RG_GROUP_W, RG_GROUP_W)


def kernel(x, meta_tokens, norm1_g, w_in, conv_w, conv_b, rg_wa, rg_ba, rg_wx, rg_bx,
           rg_lambda, hg_lb_logits, hg_norm_g, w_proj_a, w_proj_b, w_out, norm2_g,
           w_ffn_in, w_ffn_down, norm_f_g):
    nb, seq, d = x.shape
    assert d == D_MODEL and seq % TIME_TILE == 0 and (nb * seq) % FFN_ROWS == 0
    assert w_in.shape[0] == 1, "single-layer block"
    tt = TIME_TILE
    n_steps = 1 + seq // tt

    meta_tile = jnp.zeros((tt, d), x.dtype).at[tt - N_META:].set(meta_tokens.astype(x.dtype))
    w_in_b = w_in[0].astype(_BF)
    wg = jnp.concatenate([_block_diag_groups(rg_wa[0]), _block_diag_groups(rg_wx[0])],
                         axis=-1).astype(_BF)
    row = lambda v: v.reshape(1, -1).astype(_F32)

    x_map = lambda i: (0, jnp.maximum(i - 1, 0), 0)
    mixer = pl.pallas_call(
        functools.partial(_mixer_kernel, nb=nb, tt=tt),
        name="mixer",
        grid=(n_steps,),
        in_specs=[
            _resident((tt, d)),
            pl.BlockSpec((nb, tt, d), x_map),
            _small((1, d)),
            _resident(w_in_b.shape),
            _small((RG_CONV, RG_WIDTH)),
            _small((1, RG_WIDTH)),
            _resident(wg.shape),
            _small((1, RG_WIDTH)),
            _small((1, RG_WIDTH)),
            _small((1, RG_WIDTH)),
            _small((2, HG_WIDTH)),
            _small((1, HG_DK)),
            _resident((RG_WIDTH, d)),
            _resident((HG_WIDTH, d)),
            _resident((d, d)),
        ],
        out_specs=pl.BlockSpec((nb, tt, d), x_map),
        out_shape=jax.ShapeDtypeStruct((nb, seq, d), _F32),
        scratch_shapes=[
            pltpu.VMEM((nb, tt + SUBLANES, RG_WIDTH), _F32),
            pltpu.VMEM((nb, 1, RG_WIDTH), _F32),
            pltpu.VMEM((nb, HG_HEADS, HG_DK, HG_DK), _F32),
        ],
        compiler_params=pltpu.CompilerParams(
            dimension_semantics=("arbitrary",), vmem_limit_bytes=VMEM_LIMIT_BYTES),
    )
    h_mid = mixer(
        meta_tile, x, row(norm1_g[0]), w_in_b, conv_w[0].astype(_F32), row(conv_b[0]), wg,
        row(rg_ba[0]), row(rg_bx[0]), row(rg_lambda[0]), hg_lb_logits.astype(_F32),
        row(hg_norm_g[0]), w_proj_a[0].astype(_BF), w_proj_b[0].astype(_BF),
        w_out[0].astype(_BF))

    n_rows = nb * seq
    ffn = pl.pallas_call(
        _ffn_kernel,
        name="ffn",
        grid=(n_rows // FFN_ROWS,),
        in_specs=[
            pl.BlockSpec((FFN_ROWS, d), lambda i: (i, 0)),
            _small((1, d)),
            _resident((d, 2 * D_FF)),
            _resident((D_FF, d)),
            _small((1, d)),
        ],
        out_specs=pl.BlockSpec((FFN_ROWS, d), lambda i: (i, 0)),
        out_shape=jax.ShapeDtypeStruct((n_rows, d), _F32),
        compiler_params=pltpu.CompilerParams(
            dimension_semantics=("arbitrary",), vmem_limit_bytes=VMEM_LIMIT_BYTES),
    )
    out = ffn(h_mid.reshape(n_rows, d), row(norm2_g[0]), w_ffn_in[0].astype(_BF),
              w_ffn_down[0].astype(_BF), row(norm_f_g))
    return out.reshape(nb, seq, d)
```

```python
import functools
import math

import jax
import jax.numpy as jnp
from jax import lax
from jax.experimental import pallas as pl
from jax.experimental.pallas import tpu as pltpu

D_MODEL = 1024
N_META = 16
RG_WIDTH = 1280
RG_BLOCKS = 16
RG_BLOCK = 80
RG_GROUPS = 2
RG_GROUP_W = RG_WIDTH // RG_GROUPS
RG_CONV = 4
RG_C = 8.0
HG_HEADS = 8
HG_DK = 128
HG_WIDTH = 1024
D_FF = 2816
NORM_EPS = 1e-6

OFF_AX, OFF_AG, OFF_Q, OFF_F, OFF_I, OFF_G, OFF_GA, OFF_GB = (
    0, 1280, 2560, 3584, 4608, 5632, 6656, 7680)

SUBLANES = 8
TIME_TILE = 128
CHUNK = 64
FFN_ROWS = 512
FFN_COLS = 1408
VMEM_LIMIT_BYTES = 60000 * 1024

_BF = jnp.bfloat16
_F32 = jnp.float32


def _dot(a, b):
    return jnp.dot(a, b, preferred_element_type=_F32)


def _dot_nt(a, b):
    return lax.dot_general(a, b, (((1,), (1,)), ((), ())), preferred_element_type=_F32)


def _dot_tn(a, b):
    return lax.dot_general(a, b, (((0,), (0,)), ((), ())), preferred_element_type=_F32)


def _rmsnorm(x, g):
    ms = jnp.mean(x * x, axis=-1, keepdims=True)
    return x * lax.rsqrt(ms + NORM_EPS) * g


def _sigmoid(x):
    return 0.5 * jnp.tanh(0.5 * x) + 0.5


def _silu(x):
    return x * _sigmoid(x)


def _gelu_tanh(x):
    c = math.sqrt(2.0 / math.pi)
    return x * (0.5 * (1.0 + jnp.tanh(c * (x + 0.044715 * (x * x * x)))))


def _level_masks():
    t = lax.broadcasted_iota(jnp.int32, (CHUNK, CHUNK), 0)
    s = lax.broadcasted_iota(jnp.int32, (CHUNK, CHUNK), 1)
    masks = {0: t == s}
    h = 1
    while h < CHUNK:
        sh = int(math.log2(2 * h))
        masks[h] = ((t >> sh) == (s >> sh)) & ((t & h) != 0) & ((s & h) == 0)
        h *= 2
    return masks


def _hgrn_chunk(g, q, k, v_bf, st, masks):
    w = g.shape[-1]
    nv = CHUNK // SUBLANES
    sub = lax.broadcasted_iota(jnp.int32, (SUBLANES, w), 0)
    qs = [q[SUBLANES * j:SUBLANES * (j + 1)] for j in range(nv)]
    ks = [k[SUBLANES * j:SUBLANES * (j + 1)] for j in range(nv)]
    bcs = []
    run = None
    for j in range(nv):
        x = g[SUBLANES * j:SUBLANES * (j + 1)]
        for s in (1, 2, 4):
            x = x + jnp.where(sub >= s, pltpu.roll(x, s, 0), 0.0)
        if run is not None:
            x = x + run
        bcs.append(x)
        run = x[SUBLANES - 1:SUBLANES]
    b_last = run

    def ref_row(h, j):
        if h >= SUBLANES:
            r = (SUBLANES * j) // (2 * h) * (2 * h) + h - 1
            return bcs[r // SUBLANES][SUBLANES - 1:SUBLANES]
        x = bcs[j]
        rows = [jnp.broadcast_to(x[r:r + 1], x.shape)
                for r in range(h - 1, SUBLANES, 2 * h)]
        out = rows[-1]
        for idx in range(len(rows) - 2, -1, -1):
            out = jnp.where(sub < 2 * h * (idx + 1), rows[idx], out)
        return out

    levels = []
    h = CHUNK // 2
    while h >= 1:
        parts = []
        for j in range(nv):
            if h >= SUBLANES:
                base = qs[j] if (SUBLANES * j) & h else ks[j]
            else:
                base = jnp.where((sub & h) != 0, qs[j], ks[j])
            parts.append(base * jnp.exp(-jnp.abs(bcs[j] - ref_row(h, j))))
        levels.append((h, jnp.concatenate(parts, axis=0).astype(_BF)))
        h //= 2
    q_bf = q.astype(_BF)
    k_bf = k.astype(_BF)
    q_dec = jnp.concatenate([qs[j] * jnp.exp(bcs[j]) for j in range(nv)],
                            axis=0).astype(_BF)
    k_dec = jnp.concatenate([ks[j] * jnp.exp(b_last - bcs[j]) for j in range(nv)],
                            axis=0).astype(_BF)
    s_dec = jnp.exp(b_last)

    outs, new_st = [], []
    for hd in range(HG_HEADS):
        sl = slice(hd * HG_DK, (hd + 1) * HG_DK)
        scores = jnp.where(masks[0], _dot_nt(q_bf[:, sl], k_bf[:, sl]), 0.0)
        for h, x in levels:
            scores = jnp.where(masks[h], _dot_nt(x[:, sl], x[:, sl]), scores)
        o = _dot(scores.astype(_BF), v_bf[:, sl])
        o = o + _dot_nt(q_dec[:, sl], st[hd].astype(_BF))
        new_st.append(s_dec[:, sl] * st[hd] + _dot_tn(v_bf[:, sl], k_dec[:, sl]))
        outs.append(o)
    return jnp.concatenate(outs, axis=-1), new_st


def _mixer_kernel(meta_ref, x_ref, n1g_ref, w_in_ref, convw_ref, convb_ref, wg_ref,
                  ba_ref, bx_ref, lam_ref, lbl_ref, hgn_ref, wpa_ref, wpb_ref, wout_ref,
                  o_ref, cbuf, hcar, st_ref, *, nb, tt):
    i = pl.program_id(0)
    rows = nb * tt
    is0 = i == 0

    @pl.when(is0)
    def _():
        cbuf[...] = jnp.zeros_like(cbuf)
        hcar[...] = jnp.zeros_like(hcar)
        st_ref[...] = jnp.zeros_like(st_ref)

    meta = meta_ref[...]
    h_in = jnp.concatenate([jnp.where(is0, meta, x_ref[b]) for b in range(nb)], axis=0)
    hn = _rmsnorm(h_in, n1g_ref[...]).astype(_BF)

    def proj(off, width):
        return _dot(hn, w_in_ref[:, off:off + width])

    a_x = proj(OFF_AX, RG_WIDTH)
    convw = convw_ref[...]
    xc_parts = []
    for b in range(nb):
        cbuf[b, SUBLANES:, :] = a_x[b * tt:(b + 1) * tt]
        acc = convb_ref[...] + convw[RG_CONV - 1:RG_CONV] * cbuf[b, pl.ds(SUBLANES, tt), :]
        for j in range(RG_CONV - 1):
            off = SUBLANES - (RG_CONV - 1) + j
            acc = acc + convw[j:j + 1] * cbuf[b, pl.ds(off, tt), :]
        xc_parts.append(acc)
        cbuf[b, 0:SUBLANES, :] = cbuf[b, pl.ds(tt, SUBLANES), :]
    xc = jnp.concatenate(xc_parts, axis=0)
    xc_bf = xc.astype(_BF)
    r_parts, i_parts = [], []
    for gidx in range(RG_GROUPS):
        lo = gidx * RG_GROUP_W
        z = _dot(xc_bf[:, lo:lo + RG_GROUP_W], wg_ref[gidx])
        r_parts.append(_sigmoid(z[:, :RG_GROUP_W] + ba_ref[:, lo:lo + RG_GROUP_W]))
        i_parts.append(_sigmoid(z[:, RG_GROUP_W:] + bx_ref[:, lo:lo + RG_GROUP_W]))
    r_gate = jnp.concatenate(r_parts, axis=-1)
    i_gate = jnp.concatenate(i_parts, axis=-1)
    z = -lam_ref[...]
    softplus = jnp.maximum(z, 0.0) + jnp.log1p(jnp.exp(-jnp.abs(z)))
    a = jnp.exp((-RG_C) * r_gate * softplus)
    u = jnp.sqrt(1.0 - a * a) * (i_gate * xc)
    t_loc = lax.broadcasted_iota(jnp.int32, (rows, 1), 0) % tt
    u = jnp.where(jnp.logical_or(i > 0, t_loc >= tt - N_META), u, 0.0)

    sub = lax.broadcasted_iota(jnp.int32, (SUBLANES, RG_WIDTH), 0)
    h_parts = []
    for b in range(nb):
        carry = hcar[b]
        for j in range(tt // SUBLANES):
            r0 = b * tt + j * SUBLANES
            av = a[r0:r0 + SUBLANES]
            hv = u[r0:r0 + SUBLANES]
            for s in (1, 2, 4):
                keep = sub >= s
                hv = hv + av * jnp.where(keep, pltpu.roll(hv, s, 0), 0.0)
                av = av * jnp.where(keep, pltpu.roll(av, s, 0), 1.0)
            hv = hv + av * carry
            carry = hv[SUBLANES - 1:SUBLANES]
            h_parts.append(hv)
        hcar[b] = carry
    h_rg = jnp.concatenate(h_parts, axis=0)

    y_a = (h_rg * _gelu_tanh(proj(OFF_AG, RG_WIDTH))).astype(_BF)
    merged = _sigmoid(proj(OFF_GA, D_MODEL)) * _dot(y_a, wpa_ref[...])

    l0 = lbl_ref[0:1, :]
    l1 = lbl_ref[1:2, :]
    lmax = jnp.maximum(l0, l1)
    e0 = jnp.exp(l0 - lmax)
    e1 = jnp.exp(l1 - lmax)
    lb = e0 / (e0 + e1)
    f_sig = _sigmoid(proj(OFF_F, HG_WIDTH))
    log_f = jnp.log(lb + (1.0 - lb) * f_sig)
    k_in = (1.0 - lb) * (1.0 - f_sig)
    q_in = _silu(proj(OFF_Q, HG_WIDTH))
    v_in = proj(OFF_I, HG_WIDTH).astype(_BF)

    masks = _level_masks()
    o_parts = []
    for b in range(nb):
        st = [st_ref[b, hd] for hd in range(HG_HEADS)]
        for c in range(tt // CHUNK):
            r0 = b * tt + c * CHUNK
            sl = slice(r0, r0 + CHUNK)
            o, st = _hgrn_chunk(log_f[sl], q_in[sl], k_in[sl], v_in[sl], st, masks)
            o_parts.append(o)
        for hd in range(HG_HEADS):
            st_ref[b, hd] = st[hd]
    o_b = jnp.concatenate(o_parts, axis=0)

    g_act = _silu(proj(OFF_G, HG_WIDTH))
    hgn = hgn_ref[...]
    yb_parts = []
    for hd in range(HG_HEADS):
        sl = slice(hd * HG_DK, (hd + 1) * HG_DK)
        yb_parts.append((_rmsnorm(o_b[:, sl], hgn) * g_act[:, sl]).astype(_BF))
    y_b = jnp.concatenate(yb_parts, axis=-1)
    merged = merged + _sigmoid(proj(OFF_GB, D_MODEL)) * _dot(y_b, wpb_ref[...])

    out = h_in + _dot(merged.astype(_BF), wout_ref[...])
    for b in range(nb):
        o_ref[b] = out[b * tt:(b + 1) * tt]


def _ffn_kernel(h_ref, n2g_ref, wfi_ref, wfd_ref, nfg_ref, o_ref):
    h = h_ref[...]
    hn = _rmsnorm(h, n2g_ref[...]).astype(_BF)
    acc = h
    for j in range(D_FF // FFN_COLS):
        lo = j * FFN_COLS
        gcol = _dot(hn, wfi_ref[:, lo:lo + FFN_COLS])
        ucol = _dot(hn, wfi_ref[:, D_FF + lo:D_FF + lo + FFN_COLS])
        act = (_silu(gcol) * ucol).astype(_BF)
        acc = acc + _dot(act, wfd_ref[lo:lo + FFN_COLS, :])
    o_ref[...] = _rmsnorm(acc, nfg_ref[...])


def _resident(shape):
    nd = len(shape)
    return pl.BlockSpec(shape, lambda *_: (0,) * nd, pipeline_mode=pl.Buffered(1))


def _small(shape):
    nd = len(shape)
    return pl.BlockSpec(shape, lambda *_: (0,) * nd)


def _block_diag_groups(w):
    per = RG_BLOCKS // RG_GROUPS
    w = w.reshape(RG_GROUPS, per, RG_BLOCK, RG_BLOCK)
    eye = jnp.eye(per, dtype=w.dtype)
    bd = w[:, :, :, None, :] * eye[None, :, None, :, None]
    return bd.reshape(RG_GROUPS, RG_GROUP_W, RG_GROUP_W)


def kernel(x, meta_tokens, norm1_g, w_in, conv_w, conv_b, rg_wa, rg_ba, rg_wx, rg_bx,
           rg_lambda, hg_lb_logits, hg_norm_g, w_proj_a, w_proj_b, w_out, norm2_g,
           w_ffn_in, w_ffn_down, norm_f_g):
    nb, seq, d = x.shape
    assert d == D_MODEL and seq % TIME_TILE == 0 and (nb * seq) % FFN_ROWS == 0
    assert w_in.shape[0] == 1, "single-layer block"
    tt = TIME_TILE
    n_steps = 1 + seq // tt

    meta_tile = jnp.zeros((tt, d), x.dtype).at[tt - N_META:].set(meta_tokens.astype(x.dtype))
    w_in_b = w_in[0].astype(_BF)
    wg = jnp.concatenate([_block_diag_groups(rg_wa[0]), _block_diag_groups(rg_wx[0])],
                         axis=-1).astype(_BF)
    row = lambda v: v.reshape(1, -1).astype(_F32)

    x_map = lambda i: (0, jnp.maximum(i - 1, 0), 0)
    mixer = pl.pallas_call(
        functools.partial(_mixer_kernel, nb=nb, tt=tt),
        name="mixer",
        grid=(n_steps,),
        in_specs=[
            _resident((tt, d)),
            pl.BlockSpec((nb, tt, d), x_map),
            _small((1, d)),
            _resident(w_in_b.shape),
            _small((RG_CONV, RG_WIDTH)),
            _small((1, RG_WIDTH)),
            _resident(wg.shape),
            _small((1, RG_WIDTH)),
            _small((1, RG_WIDTH)),
            _small((1, RG_WIDTH)),
            _small((2, HG_WIDTH)),
            _small((1, HG_DK)),
            _resident((RG_WIDTH, d)),
            _resident((HG_WIDTH, d)),
            _resident((d, d)),
        ],
        out_specs=pl.BlockSpec((nb, tt, d), x_map),
        out_shape=jax.ShapeDtypeStruct((nb, seq, d), _F32),
        scratch_shapes=[
            pltpu.VMEM((nb, tt + SUBLANES, RG_WIDTH), _F32),
            pltpu.VMEM((nb, 1, RG_WIDTH), _F32),
            pltpu.VMEM((nb, HG_HEADS, HG_DK, HG_DK), _F32),
        ],
        compiler_params=pltpu.CompilerParams(
            dimension_semantics=("arbitrary",), vmem_limit_bytes=VMEM_LIMIT_BYTES),
    )
    h_mid = mixer(
        meta_tile, x, row(norm1_g[0]), w_in_b, conv_w[0].astype(_F32), row(conv_b[0]), wg,
        row(rg_ba[0]), row(rg_bx[0]), row(rg_lambda[0]), hg_lb_logits.astype(_F32),
        row(hg_norm_g[0]), w_proj_a[0].astype(_BF), w_proj_b[0].astype(_BF),
        w_out[0].astype(_BF))

    n_rows = nb * seq
    ffn = pl.pallas_call(
        _ffn_kernel,
        name="ffn",
        grid=(n_rows // FFN_ROWS,),
        in_specs=[
            pl.BlockSpec((FFN_ROWS, d), lambda i: (i, 0)),
            _small((1, d)),
            _resident((d, 2 * D_FF)),
            _resident((D_FF, d)),
            _small((1, d)),
        ],
        out_specs=pl.BlockSpec((FFN_ROWS, d), lambda i: (i, 0)),
        out_shape=jax.ShapeDtypeStruct((n_rows, d), _F32),
        compiler_params=pltpu.CompilerParams(
            dimension_semantics=("arbitrary",), vmem_limit_bytes=VMEM_LIMIT_BYTES),
    )
    out = ffn(h_mid.reshape(n_rows, d), row(norm2_g[0]), w_ffn_in[0].astype(_BF),
              w_ffn_down[0].astype(_BF), row(norm_f_g))
    return out.reshape(nb, seq, d)
```

```python
import functools
import math

import jax
import jax.numpy as jnp
from jax import lax
from jax.experimental import pallas as pl
from jax.experimental.pallas import tpu as pltpu

D_MODEL = 1024
N_META = 16
RG_WIDTH = 1280
RG_BLOCKS = 16
RG_BLOCK = 80
RG_GROUPS = 2
RG_GROUP_W = RG_WIDTH // RG_GROUPS
RG_CONV = 4
RG_C = 8.0
HG_HEADS = 8
HG_DK = 128
HG_WIDTH = 1024
D_FF = 2816
NORM_EPS = 1e-6

OFF_AX, OFF_AG, OFF_Q, OFF_F, OFF_I, OFF_G, OFF_GA, OFF_GB = (
    0, 1280, 2560, 3584, 4608, 5632, 6656, 7680)

SUBLANES = 8
TIME_TILE = 128
CHUNK = 64
PROJ_COLS = 512
FFN_ROWS = 512
FFN_COLS = 1408
VMEM_LIMIT_BYTES = 60000 * 1024

_BF = jnp.bfloat16
_F32 = jnp.float32


def _dot(a, b):
    return jnp.dot(a, b, preferred_element_type=_F32)


def _dot_nt(a, b):
    return lax.dot_general(a, b, (((1,), (1,)), ((), ())), preferred_element_type=_F32)


def _dot_tn(a, b):
    return lax.dot_general(a, b, (((0,), (0,)), ((), ())), preferred_element_type=_F32)


def _rmsnorm(x, g):
    ms = jnp.mean(x * x, axis=-1, keepdims=True)
    return x * lax.rsqrt(ms + NORM_EPS) * g


def _sigmoid(x):
    return 0.5 * jnp.tanh(0.5 * x) + 0.5


def _silu(x):
    return x * _sigmoid(x)


def _gelu_tanh(x):
    c = math.sqrt(2.0 / math.pi)
    return x * (0.5 * (1.0 + jnp.tanh(c * (x + 0.044715 * (x * x * x)))))


def _level_masks():
    t = lax.broadcasted_iota(jnp.int32, (CHUNK, CHUNK), 0)
    s = lax.broadcasted_iota(jnp.int32, (CHUNK, CHUNK), 1)
    masks = {0: t == s}
    h = 1
    while h < CHUNK:
        sh = int(math.log2(2 * h))
        masks[h] = ((t >> sh) == (s >> sh)) & ((t & h) != 0) & ((s & h) == 0)
        h *= 2
    return masks


def _hgrn_chunk(g, q, k, v_bf, st, masks, tick):
    w = g.shape[-1]
    nv = CHUNK // SUBLANES
    sub = lax.broadcasted_iota(jnp.int32, (SUBLANES, w), 0)
    qs = [q[SUBLANES * j:SUBLANES * (j + 1)] for j in range(nv)]
    ks = [k[SUBLANES * j:SUBLANES * (j + 1)] for j in range(nv)]
    bcs = []
    run = None
    for j in range(nv):
        x = g[SUBLANES * j:SUBLANES * (j + 1)]
        for s in (1, 2, 4):
            x = x + jnp.where(sub >= s, pltpu.roll(x, s, 0), 0.0)
        if run is not None:
            x = x + run
        bcs.append(x)
        run = x[SUBLANES - 1:SUBLANES]
    b_last = run

    def ref_row(h, j):
        if h >= SUBLANES:
            r = (SUBLANES * j) // (2 * h) * (2 * h) + h - 1
            return bcs[r // SUBLANES][SUBLANES - 1:SUBLANES]
        x = bcs[j]
        rows = [jnp.broadcast_to(x[r:r + 1], x.shape)
                for r in range(h - 1, SUBLANES, 2 * h)]
        out = rows[-1]
        for idx in range(len(rows) - 2, -1, -1):
            out = jnp.where(sub < 2 * h * (idx + 1), rows[idx], out)
        return out

    levels = []
    h = CHUNK // 2
    while h >= 1:
        parts = []
        for j in range(nv):
            if h >= SUBLANES:
                base = qs[j] if (SUBLANES * j) & h else ks[j]
            else:
                base = jnp.where((sub & h) != 0, qs[j], ks[j])
            parts.append(base * jnp.exp(-jnp.abs(bcs[j] - ref_row(h, j))))
        levels.append((h, jnp.concatenate(parts, axis=0).astype(_BF)))
        h //= 2
    q_bf = q.astype(_BF)
    k_bf = k.astype(_BF)
    q_dec = jnp.concatenate([qs[j] * jnp.exp(bcs[j]) for j in range(nv)],
                            axis=0).astype(_BF)
    k_dec = jnp.concatenate([ks[j] * jnp.exp(b_last - bcs[j]) for j in range(nv)],
                            axis=0).astype(_BF)
    s_dec = jnp.exp(b_last)

    heads = [slice(hd * HG_DK, (hd + 1) * HG_DK) for hd in range(HG_HEADS)]
    all_scores = []
    for hd, sl in enumerate(heads):
        scores = jnp.where(masks[0], _dot_nt(q_bf[:, sl], k_bf[:, sl]), 0.0)
        for h, x in levels:
            scores = jnp.where(masks[h], _dot_nt(x[:, sl], x[:, sl]), scores)
        all_scores.append(scores.astype(_BF))
        if hd % 4 == 3:
            tick()
    outs, new_st = [], []
    for hd, sl in enumerate(heads):
        o = _dot(all_scores[hd], v_bf[:, sl])
        o = o + _dot_nt(q_dec[:, sl], st[hd].astype(_BF))
        new_st.append(s_dec[:, sl] * st[hd] + _dot_tn(v_bf[:, sl], k_dec[:, sl]))
        outs.append(o)
    return jnp.concatenate(outs, axis=-1), new_st


def _mixer_kernel(meta_ref, xa_ref, xb_ref, n1g_ref, w_in_ref, convw_ref, convb_ref, wg_ref,
                  ba_ref, bx_ref, lam_ref, lbl_ref, hgn_ref, wpa_ref, wpb_ref, wout_ref,
                  o_ref, p_ref, cbuf, hcar, st_ref, *, nb, tt):
    i = pl.program_id(0)
    rows = nb * tt

    @pl.when(i == 0)
    def _():
        p_ref[...] = jnp.zeros_like(p_ref)

    @pl.when(i <= 1)
    def _():
        cbuf[...] = jnp.zeros_like(cbuf)
        hcar[...] = jnp.zeros_like(hcar)
        st_ref[...] = jnp.zeros_like(st_ref)

    meta = meta_ref[...]

    def saved(off, width):
        return p_ref[:, off:off + width]

    a_x = saved(OFF_AX, RG_WIDTH)
    a_gate = saved(OFF_AG, RG_WIDTH)
    q_raw = saved(OFF_Q, HG_WIDTH)
    f_logit = saved(OFF_F, HG_WIDTH)
    v_raw = saved(OFF_I, HG_WIDTH)
    g_raw = saved(OFF_G, HG_WIDTH)
    ga_raw = saved(OFF_GA, D_MODEL)
    gb_raw = saved(OFF_GB, D_MODEL)

    h_a = jnp.concatenate([jnp.where(i == 0, meta, xa_ref[b]) for b in range(nb)], axis=0)
    hn = _rmsnorm(h_a, n1g_ref[...]).astype(_BF)
    a_cols = iter(range(0, w_in_ref.shape[1], PROJ_COLS))

    def project_next(n=1):
        for _ in range(n):
            off = next(a_cols, None)
            if off is not None:
                p_ref[:, off:off + PROJ_COLS] = _dot(hn, w_in_ref[:, off:off + PROJ_COLS])

    project_next()

    h_in = jnp.concatenate([jnp.where(i == 1, meta, xb_ref[b]) for b in range(nb)], axis=0)
    convw = convw_ref[...]
    xc_parts = []
    for b in range(nb):
        cbuf[b, SUBLANES:, :] = a_x[b * tt:(b + 1) * tt]
        acc = convb_ref[...] + convw[RG_CONV - 1:RG_CONV] * cbuf[b, pl.ds(SUBLANES, tt), :]
        for j in range(RG_CONV - 1):
            off = SUBLANES - (RG_CONV - 1) + j
            acc = acc + convw[j:j + 1] * cbuf[b, pl.ds(off, tt), :]
        xc_parts.append(acc)
        project_next()
        cbuf[b, 0:SUBLANES, :] = cbuf[b, pl.ds(tt, SUBLANES), :]
    xc = jnp.concatenate(xc_parts, axis=0)
    xc_bf = xc.astype(_BF)
    r_parts, i_parts = [], []
    for gidx in range(RG_GROUPS):
        lo = gidx * RG_GROUP_W
        z = _dot(xc_bf[:, lo:lo + RG_GROUP_W], wg_ref[gidx])
        r_parts.append(_sigmoid(z[:, :RG_GROUP_W] + ba_ref[:, lo:lo + RG_GROUP_W]))
        i_parts.append(_sigmoid(z[:, RG_GROUP_W:] + bx_ref[:, lo:lo + RG_GROUP_W]))
        project_next()
    r_gate = jnp.concatenate(r_parts, axis=-1)
    i_gate = jnp.concatenate(i_parts, axis=-1)
    z = -lam_ref[...]
    softplus = jnp.maximum(z, 0.0) + jnp.log1p(jnp.exp(-jnp.abs(z)))
    a = jnp.exp((-RG_C) * r_gate * softplus)
    u = jnp.sqrt(1.0 - a * a) * (i_gate * xc)
    t_loc = lax.broadcasted_iota(jnp.int32, (rows, 1), 0) % tt
    u = jnp.where(jnp.logical_or(i != 1, t_loc >= tt - N_META), u, 0.0)
    project_next()

    sub = lax.broadcasted_iota(jnp.int32, (SUBLANES, RG_WIDTH), 0)
    h_parts = []
    for b in range(nb):
        carry = hcar[b]
        for j in range(tt // SUBLANES):
            r0 = b * tt + j * SUBLANES
            av = a[r0:r0 + SUBLANES]
            hv = u[r0:r0 + SUBLANES]
            for s in (1, 2, 4):
                keep = sub >= s
                hv = hv + av * jnp.where(keep, pltpu.roll(hv, s, 0), 0.0)
                av = av * jnp.where(keep, pltpu.roll(av, s, 0), 1.0)
            hv = hv + av * carry
            carry = hv[SUBLANES - 1:SUBLANES]
            h_parts.append(hv)
            if j % 8 == 7:
                project_next()
        hcar[b] = carry
    h_rg = jnp.concatenate(h_parts, axis=0)

    y_a = (h_rg * _gelu_tanh(a_gate)).astype(_BF)
    project_next()
    merged = _sigmoid(ga_raw) * _dot(y_a, wpa_ref[...])

    l0 = lbl_ref[0:1, :]
    l1 = lbl_ref[1:2, :]
    lmax = jnp.maximum(l0, l1)
    e0 = jnp.exp(l0 - lmax)
    e1 = jnp.exp(l1 - lmax)
    lb = e0 / (e0 + e1)
    f_sig = _sigmoid(f_logit)
    log_f = jnp.log(lb + (1.0 - lb) * f_sig)
    k_in = (1.0 - lb) * (1.0 - f_sig)
    q_in = _silu(q_raw)
    project_next()
    v_in = v_raw.astype(_BF)

    masks = _level_masks()
    o_parts = []
    for b in range(nb):
        st = [st_ref[b, hd] for hd in range(HG_HEADS)]
        for c in range(tt // CHUNK):
            r0 = b * tt + c * CHUNK
            sl = slice(r0, r0 + CHUNK)
            o, st = _hgrn_chunk(log_f[sl], q_in[sl], k_in[sl], v_in[sl], st, masks,
                                project_next)
            o_parts.append(o)
        for hd in range(HG_HEADS):
            st_ref[b, hd] = st[hd]
    o_b = jnp.concatenate(o_parts, axis=0)

    project_next(w_in_ref.shape[1] // PROJ_COLS)
    g_act = _silu(g_raw)
    hgn = hgn_ref[...]
    yb_parts = []
    for hd in range(HG_HEADS):
        sl = slice(hd * HG_DK, (hd + 1) * HG_DK)
        yb_parts.append((_rmsnorm(o_b[:, sl], hgn) * g_act[:, sl]).astype(_BF))
    y_b = jnp.concatenate(yb_parts, axis=-1)
    merged = merged + _sigmoid(gb_raw) * _dot(y_b, wpb_ref[...])

    out = h_in + _dot(merged.astype(_BF), wout_ref[...])
    for b in range(nb):
        o_ref[b] = out[b * tt:(b + 1) * tt]


def _ffn_kernel(h_ref, n2g_ref, wfi_ref, wfd_ref, nfg_ref, o_ref):
    h = h_ref[...]
    hn = _rmsnorm(h, n2g_ref[...]).astype(_BF)
    acc = h
    for j in range(D_FF // FFN_COLS):
        lo = j * FFN_COLS
        gcol = _dot(hn, wfi_ref[:, lo:lo + FFN_COLS])
        ucol = _dot(hn, wfi_ref[:, D_FF + lo:D_FF + lo + FFN_COLS])
        act = (_silu(gcol) * ucol).astype(_BF)
        acc = acc + _dot(act, wfd_ref[lo:lo + FFN_COLS, :])
    o_ref[...] = _rmsnorm(acc, nfg_ref[...])


def _resident(shape):
    nd = len(shape)
    return pl.BlockSpec(shape, lambda *_: (0,) * nd, pipeline_mode=pl.Buffered(1))


def _small(shape):
    nd = len(shape)
    return pl.BlockSpec(shape, lambda *_: (0,) * nd)


def _block_diag_groups(w):
    per = RG_BLOCKS // RG_GROUPS
    w = w.reshape(RG_GROUPS, per, RG_BLOCK, RG_BLOCK)
    eye = jnp.eye(per, dtype=w.dtype)
    bd = w[:, :, :, None, :] * eye[None, :, None, :, None]
    return bd.reshape(RG_GROUPS, RG_GROUP_W, RG_GROUP_W)


def kernel(x, meta_tokens, norm1_g, w_in, conv_w, conv_b, rg_wa, rg_ba, rg_wx, rg_bx,
           rg_lambda, hg_lb_logits, hg_norm_g, w_proj_a, w_proj_b, w_out, norm2_g,
           w_ffn_in, w_ffn_down, norm_f_g):
    nb, seq, d = x.shape
    assert d == D_MODEL and seq % TIME_TILE == 0 and (nb * seq) % FFN_ROWS == 0
    assert w_in.shape[0] == 1, "single-layer block"
    tt = TIME_TILE
    n_blocks = seq // tt
    n_steps = n_blocks + 2

    meta_tile = jnp.zeros((tt, d), x.dtype).at[tt - N_META:].set(meta_tokens.astype(x.dtype))
    w_in_b = w_in[0].astype(_BF)
    wg = jnp.concatenate([_block_diag_groups(rg_wa[0]), _block_diag_groups(rg_wx[0])],
                         axis=-1).astype(_BF)
    row = lambda v: v.reshape(1, -1).astype(_F32)

    xa_map = lambda i: (0, jnp.clip(i - 1, 0, n_blocks - 1), 0)
    xb_map = lambda i: (0, jnp.clip(i - 2, 0, n_blocks - 1), 0)
    mixer = pl.pallas_call(
        functools.partial(_mixer_kernel, nb=nb, tt=tt),
        name="mixer",
        grid=(n_steps,),
        in_specs=[
            _resident((tt, d)),
            pl.BlockSpec((nb, tt, d), xa_map),
            pl.BlockSpec((nb, tt, d), xb_map),
            _small((1, d)),
            _resident(w_in_b.shape),
            _small((RG_CONV, RG_WIDTH)),
            _small((1, RG_WIDTH)),
            _resident(wg.shape),
            _small((1, RG_WIDTH)),
            _small((1, RG_WIDTH)),
            _small((1, RG_WIDTH)),
            _small((2, HG_WIDTH)),
            _small((1, HG_DK)),
            _resident((RG_WIDTH, d)),
            _resident((HG_WIDTH, d)),
            _resident((d, d)),
        ],
        out_specs=pl.BlockSpec((nb, tt, d), xb_map),
        out_shape=jax.ShapeDtypeStruct((nb, seq, d), _F32),
        scratch_shapes=[
            pltpu.VMEM((nb * tt, w_in_b.shape[1]), _F32),
            pltpu.VMEM((nb, tt + SUBLANES, RG_WIDTH), _F32),
            pltpu.VMEM((nb, 1, RG_WIDTH), _F32),
            pltpu.VMEM((nb, HG_HEADS, HG_DK, HG_DK), _F32),
        ],
        compiler_params=pltpu.CompilerParams(
            dimension_semantics=("arbitrary",), vmem_limit_bytes=VMEM_LIMIT_BYTES),
    )
    h_mid = mixer(
        meta_tile, x, x, row(norm1_g[0]), w_in_b, conv_w[0].astype(_F32), row(conv_b[0]), wg,
        row(rg_ba[0]), row(rg_bx[0]), row(rg_lambda[0]), hg_lb_logits.astype(_F32),
        row(hg_norm_g[0]), w_proj_a[0].astype(_BF), w_proj_b[0].astype(_BF),
        w_out[0].astype(_BF))

    n_rows = nb * seq
    ffn = pl.pallas_call(
        _ffn_kernel,
        name="ffn",
        grid=(n_rows // FFN_ROWS,),
        in_specs=[
            pl.BlockSpec((FFN_ROWS, d), lambda i: (i, 0)),
            _small((1, d)),
            _resident((d, 2 * D_FF)),
            _resident((D_FF, d)),
            _small((1, d)),
        ],
        out_specs=pl.BlockSpec((FFN_ROWS, d), lambda i: (i, 0)),
        out_shape=jax.ShapeDtypeStruct((n_rows, d), _F32),
        compiler_params=pltpu.CompilerParams(
            dimension_semantics=("arbitrary",), vmem_limit_bytes=VMEM_LIMIT_BYTES),
    )
    out = ffn(h_mid.reshape(n_rows, d), row(norm2_g[0]), w_ffn_in[0].astype(_BF),
              w_ffn_down[0].astype(_BF), row(norm_f_g))
    return out.reshape(nb, seq, d)
```

```python
import functools
import math

import jax
import jax.numpy as jnp
from jax import lax
from jax.experimental import pallas as pl
from jax.experimental.pallas import tpu as pltpu

D_MODEL = 1024
N_META = 16
RG_WIDTH = 1280
RG_BLOCKS = 16
RG_BLOCK = 80
RG_GROUPS = 2
RG_GROUP_W = RG_WIDTH // RG_GROUPS
RG_CONV = 4
RG_C = 8.0
HG_HEADS = 8
HG_DK = 128
HG_WIDTH = 1024
D_FF = 2816
NORM_EPS = 1e-6
LOG2_E = math.log2(math.e)

OFF_AX, OFF_AG, OFF_Q, OFF_F, OFF_I, OFF_G, OFF_GA, OFF_GB = (
    0, 1280, 2560, 3584, 4608, 5632, 6656, 7680)

SUBLANES = 8
TIME_TILE = 128
CHUNK = 64
PROJ_COLS = 512
FFN_ROWS = 512
FFN_COLS = 1408
VMEM_LIMIT_BYTES = 60000 * 1024

_BF = jnp.bfloat16
_F32 = jnp.float32


def _dot(a, b):
    return jnp.dot(a, b, preferred_element_type=_F32)


def _dot_nt(a, b):
    return lax.dot_general(a, b, (((1,), (1,)), ((), ())), preferred_element_type=_F32)


def _dot_tn(a, b):
    return lax.dot_general(a, b, (((0,), (0,)), ((), ())), preferred_element_type=_F32)


def _rmsnorm(x, g):
    ms = jnp.mean(x * x, axis=-1, keepdims=True)
    return x * lax.rsqrt(ms + NORM_EPS) * g


def _sigmoid(x):
    return 0.5 * jnp.tanh(0.5 * x) + 0.5


def _silu(x):
    hx = 0.5 * x
    return hx * jnp.tanh(hx) + hx


def _sqrt_nonneg(x):
    return jnp.where(x > 0.0, x * lax.rsqrt(x), 0.0)


def _gelu_tanh(x):
    c = math.sqrt(2.0 / math.pi)
    return x * (0.5 * (1.0 + jnp.tanh(c * (x + 0.044715 * (x * x * x)))))


def _level_masks():
    t = lax.broadcasted_iota(jnp.int32, (CHUNK, CHUNK), 0)
    s = lax.broadcasted_iota(jnp.int32, (CHUNK, CHUNK), 1)
    masks = {0: t == s}
    h = 1
    while h < CHUNK:
        sh = int(math.log2(2 * h))
        masks[h] = ((t >> sh) == (s >> sh)) & ((t & h) != 0) & ((s & h) == 0)
        h *= 2
    return masks


def _hgrn_scores(g2, q, k, masks, tick):
    w = g2.shape[-1]
    nv = CHUNK // SUBLANES
    sub = lax.broadcasted_iota(jnp.int32, (SUBLANES, w), 0)
    qs = [q[SUBLANES * j:SUBLANES * (j + 1)] for j in range(nv)]
    ks = [k[SUBLANES * j:SUBLANES * (j + 1)] for j in range(nv)]
    bcs = []
    run = None
    for j in range(nv):
        x = g2[SUBLANES * j:SUBLANES * (j + 1)]
        for s in (1, 2, 4):
            x = x + jnp.where(sub >= s, pltpu.roll(x, s, 0), 0.0)
        if run is not None:
            x = x + run
        bcs.append(x)
        run = x[SUBLANES - 1:SUBLANES]
    b_last = run

    def small_ref_row(h, j):
        x = bcs[j]
        rows = [jnp.broadcast_to(x[r:r + 1], x.shape)
                for r in range(h - 1, SUBLANES, 2 * h)]
        out = rows[-1]
        for idx in range(len(rows) - 2, -1, -1):
            out = jnp.where(sub < 2 * h * (idx + 1), rows[idx], out)
        return out

    levels = []
    h = CHUNK // 2
    while h >= 1:
        parts = []
        for j in range(nv):
            if h >= SUBLANES:
                r = (SUBLANES * j) // (2 * h) * (2 * h) + h - 1
                ref = bcs[r // SUBLANES][SUBLANES - 1:SUBLANES]
                if (SUBLANES * j) & h:
                    parts.append(qs[j] * jnp.exp2(bcs[j] - ref))
                else:
                    parts.append(ks[j] * jnp.exp2(ref - bcs[j]))
            else:
                base = jnp.where((sub & h) != 0, qs[j], ks[j])
                parts.append(base * jnp.exp2(-jnp.abs(bcs[j] - small_ref_row(h, j))))
        levels.append((h, jnp.concatenate(parts, axis=0).astype(_BF)))
        h //= 2
    q_bf = q.astype(_BF)
    k_bf = k.astype(_BF)
    q_dec = jnp.concatenate([qs[j] * jnp.exp2(bcs[j]) for j in range(nv)],
                            axis=0).astype(_BF)
    k_dec = jnp.concatenate([ks[j] * jnp.exp2(b_last - bcs[j]) for j in range(nv)],
                            axis=0).astype(_BF)
    s_dec = jnp.exp2(b_last)

    all_scores = []
    for hd in range(HG_HEADS):
        sl = slice(hd * HG_DK, (hd + 1) * HG_DK)
        scores = jnp.where(masks[0], _dot_nt(q_bf[:, sl], k_bf[:, sl]), 0.0)
        for h, x in levels:
            scores = jnp.where(masks[h], _dot_nt(x[:, sl], x[:, sl]), scores)
        all_scores.append(scores.astype(_BF))
        if hd % 4 == 3:
            tick()
    return all_scores, q_dec, k_dec, s_dec


def _hgrn_apply(phase1, v_bf, st):
    all_scores, q_dec, k_dec, s_dec = phase1
    outs, new_st = [], []
    for hd in range(HG_HEADS):
        sl = slice(hd * HG_DK, (hd + 1) * HG_DK)
        o = _dot(all_scores[hd], v_bf[:, sl])
        o = o + _dot_nt(q_dec[:, sl], st[hd].astype(_BF))
        new_st.append(s_dec[:, sl] * st[hd] + _dot_tn(v_bf[:, sl], k_dec[:, sl]))
        outs.append(o)
    return jnp.concatenate(outs, axis=-1), new_st


def _projection_chunks():
    plan = (("ax", OFF_AX, RG_WIDTH), ("f", OFF_F, HG_WIDTH), ("q", OFF_Q, HG_WIDTH),
            ("v", OFF_I, HG_WIDTH), ("ag", OFF_AG, RG_WIDTH), ("ga", OFF_GA, D_MODEL),
            ("g", OFF_G, HG_WIDTH), ("gb", OFF_GB, D_MODEL))
    chunks = []
    for name, off, width in plan:
        for c in range(0, width, PROJ_COLS):
            chunks.append((name, off + c, min(PROJ_COLS, width - c)))
    return chunks


def _mixer_kernel(meta_ref, x_ref, n1g_ref, w_in_ref, convw_ref, convb_ref, wg_ref,
                  ba_ref, bx_ref, lam_ref, lbl_ref, hgn_ref, wpa_ref, wpb_ref, wout_ref,
                  o_ref, cbuf, hcar, st_ref, *, nb, tt):
    i = pl.program_id(0)
    rows = nb * tt
    is_meta = i == 0

    @pl.when(is_meta)
    def _():
        cbuf[...] = jnp.zeros_like(cbuf)
        hcar[...] = jnp.zeros_like(hcar)
        st_ref[...] = jnp.zeros_like(st_ref)

    meta = meta_ref[...]
    h_in = jnp.concatenate([jnp.where(is_meta, meta, x_ref[b]) for b in range(nb)], axis=0)
    hn = _rmsnorm(h_in, n1g_ref[...]).astype(_BF)

    chunks = _projection_chunks()
    issued = {}
    cursor = iter(chunks)

    def project_next(n=1):
        for _ in range(n):
            nxt = next(cursor, None)
            if nxt is None:
                return
            name, off, width = nxt
            issued.setdefault(name, []).append(_dot(hn, w_in_ref[:, off:off + width]))

    def projected(name):
        total = sum(1 for c in chunks if c[0] == name)
        while len(issued.get(name, ())) < total:
            project_next()
        return jnp.concatenate(issued[name], axis=-1)

    a_x = projected("ax")

    convw = convw_ref[...]
    xc_parts = []
    for b in range(nb):
        cbuf[b, SUBLANES:, :] = a_x[b * tt:(b + 1) * tt]
        acc = convb_ref[...] + convw[RG_CONV - 1:RG_CONV] * cbuf[b, pl.ds(SUBLANES, tt), :]
        for j in range(RG_CONV - 1):
            off = SUBLANES - (RG_CONV - 1) + j
            acc = acc + convw[j:j + 1] * cbuf[b, pl.ds(off, tt), :]
        xc_parts.append(acc)
        project_next()
        cbuf[b, 0:SUBLANES, :] = cbuf[b, pl.ds(tt, SUBLANES), :]
    xc = jnp.concatenate(xc_parts, axis=0)
    xc_bf = xc.astype(_BF)
    z = -lam_ref[...]
    softplus = jnp.maximum(z, 0.0) + jnp.log1p(jnp.exp(-jnp.abs(z)))
    a_scale = (-0.5 * RG_C * LOG2_E) * softplus
    a_parts, u_parts = [], []
    for gidx in range(RG_GROUPS):
        lo = gidx * RG_GROUP_W
        cols = slice(lo, lo + RG_GROUP_W)
        zz = _dot(xc_bf[:, cols], wg_ref[gidx])
        r_tanh = jnp.tanh(0.5 * (zz[:, :RG_GROUP_W] + ba_ref[:, cols]))
        a_g = jnp.exp2(a_scale[:, cols] * r_tanh + a_scale[:, cols])
        i_g = _sigmoid(zz[:, RG_GROUP_W:] + bx_ref[:, cols])
        a_parts.append(a_g)
        u_parts.append(_sqrt_nonneg(1.0 - a_g * a_g) * (i_g * xc[:, cols]))
        project_next()
    a = jnp.concatenate(a_parts, axis=-1)
    u = jnp.concatenate(u_parts, axis=-1)
    t_loc = lax.broadcasted_iota(jnp.int32, (rows, 1), 0) % tt
    u = jnp.where(jnp.logical_or(i > 0, t_loc >= tt - N_META), u, 0.0)
    project_next()

    sub = lax.broadcasted_iota(jnp.int32, (SUBLANES, RG_WIDTH), 0)
    h_parts = []
    for b in range(nb):
        carry = hcar[b]
        for j in range(tt // SUBLANES):
            r0 = b * tt + j * SUBLANES
            av = a[r0:r0 + SUBLANES]
            hv = u[r0:r0 + SUBLANES]
            for s in (1, 2, 4):
                keep = sub >= s
                hv = hv + av * jnp.where(keep, pltpu.roll(hv, s, 0), 0.0)
                av = av * jnp.where(keep, pltpu.roll(av, s, 0), 1.0)
            hv = hv + av * carry
            carry = hv[SUBLANES - 1:SUBLANES]
            h_parts.append(hv)
            if j % 8 == 7:
                project_next()
        hcar[b] = carry
    h_rg = jnp.concatenate(h_parts, axis=0)

    y_a = (h_rg * _gelu_tanh(projected("ag"))).astype(_BF)
    project_next()
    merged = _sigmoid(projected("ga")) * _dot(y_a, wpa_ref[...])

    l0 = lbl_ref[0:1, :]
    l1 = lbl_ref[1:2, :]
    lmax = jnp.maximum(l0, l1)
    e0 = jnp.exp(l0 - lmax)
    e1 = jnp.exp(l1 - lmax)
    lb = e0 / (e0 + e1)
    f_sig = _sigmoid(projected("f"))
    log2_f = jnp.log(lb + (1.0 - lb) * f_sig) * LOG2_E
    k_in = (1.0 - lb) * (1.0 - f_sig)
    q_in = _silu(projected("q"))
    project_next()
    v_in = projected("v").astype(_BF)

    masks = _level_masks()
    order = [(b, c) for c in range(tt // CHUNK) for b in range(nb)]
    st = {b: [st_ref[b, hd] for hd in range(HG_HEADS)] for b in range(nb)}
    o_chunks = {}

    def rows_of(b, c):
        r0 = b * tt + c * CHUNK
        return slice(r0, r0 + CHUNK)

    def apply_phase(b, c, phase1):
        o_chunks[(b, c)], st[b] = _hgrn_apply(phase1, v_in[rows_of(b, c)], st[b])

    pending = None
    for b, c in order:
        sl = rows_of(b, c)
        phase1 = _hgrn_scores(log2_f[sl], q_in[sl], k_in[sl], masks, project_next)
        if pending is not None:
            apply_phase(*pending)
        pending = (b, c, phase1)
    apply_phase(*pending)
    for b in range(nb):
        for hd in range(HG_HEADS):
            st_ref[b, hd] = st[b][hd]
    o_b = jnp.concatenate([o_chunks[(b, c)] for b in range(nb) for c in range(tt // CHUNK)],
                          axis=0)

    g_act = _silu(projected("g"))
    hgn = hgn_ref[...]
    yb_parts = []
    for hd in range(HG_HEADS):
        sl = slice(hd * HG_DK, (hd + 1) * HG_DK)
        yb_parts.append((_rmsnorm(o_b[:, sl], hgn) * g_act[:, sl]).astype(_BF))
    y_b = jnp.concatenate(yb_parts, axis=-1)
    merged = merged + _sigmoid(projected("gb")) * _dot(y_b, wpb_ref[...])

    out = h_in + _dot(merged.astype(_BF), wout_ref[...])
    for b in range(nb):
        o_ref[b] = out[b * tt:(b + 1) * tt]


def _ffn_kernel(h_ref, n2g_ref, wfi_ref, wfd_ref, nfg_ref, o_ref):
    h = h_ref[...]
    hn = _rmsnorm(h, n2g_ref[...]).astype(_BF)
    acc = h
    for j in range(D_FF // FFN_COLS):
        lo = j * FFN_COLS
        gcol = _dot(hn, wfi_ref[:, lo:lo + FFN_COLS])
        ucol = _dot(hn, wfi_ref[:, D_FF + lo:D_FF + lo + FFN_COLS])
        act = (_silu(gcol) * ucol).astype(_BF)
        acc = acc + _dot(act, wfd_ref[lo:lo + FFN_COLS, :])
    o_ref[...] = _rmsnorm(acc, nfg_ref[...])


def _resident(shape):
    nd = len(shape)
    return pl.BlockSpec(shape, lambda *_: (0,) * nd, pipeline_mode=pl.Buffered(1))


def _small(shape):
    nd = len(shape)
    return pl.BlockSpec(shape, lambda *_: (0,) * nd)


def _block_diag_groups(w):
    per = RG_BLOCKS // RG_GROUPS
    w = w.reshape(RG_GROUPS, per, RG_BLOCK, RG_BLOCK)
    eye = jnp.eye(per, dtype=w.dtype)
    bd = w[:, :, :, None, :] * eye[None, :, None, :, None]
    return bd.reshape(RG_GROUPS, RG_GROUP_W, RG_GROUP_W)


def kernel(x, meta_tokens, norm1_g, w_in, conv_w, conv_b, rg_wa, rg_ba, rg_wx, rg_bx,
           rg_lambda, hg_lb_logits, hg_norm_g, w_proj_a, w_proj_b, w_out, norm2_g,
           w_ffn_in, w_ffn_down, norm_f_g):
    nb, seq, d = x.shape
    assert d == D_MODEL and seq % TIME_TILE == 0 and (nb * seq) % FFN_ROWS == 0
    assert w_in.shape[0] == 1, "single-layer block"
    tt = TIME_TILE
    n_steps = 1 + seq // tt

    meta_tile = jnp.zeros((tt, d), x.dtype).at[tt - N_META:].set(meta_tokens.astype(x.dtype))
    w_in_b = w_in[0].astype(_BF)
    wg = jnp.concatenate([_block_diag_groups(rg_wa[0]), _block_diag_groups(rg_wx[0])],
                         axis=-1).astype(_BF)
    row = lambda v: v.reshape(1, -1).astype(_F32)

    x_map = lambda i: (0, jnp.maximum(i - 1, 0), 0)
    mixer = pl.pallas_call(
        functools.partial(_mixer_kernel, nb=nb, tt=tt),
        name="mixer",
        grid=(n_steps,),
        in_specs=[
            _resident((tt, d)),
            pl.BlockSpec((nb, tt, d), x_map),
            _small((1, d)),
            _resident(w_in_b.shape),
            _small((RG_CONV, RG_WIDTH)),
            _small((1, RG_WIDTH)),
            _resident(wg.shape),
            _small((1, RG_WIDTH)),
            _small((1, RG_WIDTH)),
            _small((1, RG_WIDTH)),
            _small((2, HG_WIDTH)),
            _small((1, HG_DK)),
            _resident((RG_WIDTH, d)),
            _resident((HG_WIDTH, d)),
            _resident((d, d)),
        ],
        out_specs=pl.BlockSpec((nb, tt, d), x_map),
        out_shape=jax.ShapeDtypeStruct((nb, seq, d), _F32),
        scratch_shapes=[
            pltpu.VMEM((nb, tt + SUBLANES, RG_WIDTH), _F32),
            pltpu.VMEM((nb, 1, RG_WIDTH), _F32),
            pltpu.VMEM((nb, HG_HEADS, HG_DK, HG_DK), _F32),
        ],
        compiler_params=pltpu.CompilerParams(
            dimension_semantics=("arbitrary",), vmem_limit_bytes=VMEM_LIMIT_BYTES),
    )
    h_mid = mixer(
        meta_tile, x, row(norm1_g[0]), w_in_b, conv_w[0].astype(_F32), row(conv_b[0]), wg,
        row(rg_ba[0]), row(rg_bx[0]), row(rg_lambda[0]), hg_lb_logits.astype(_F32),
        row(hg_norm_g[0]), w_proj_a[0].astype(_BF), w_proj_b[0].astype(_BF),
        w_out[0].astype(_BF))

    n_rows = nb * seq
    ffn = pl.pallas_call(
        _ffn_kernel,
        name="ffn",
        grid=(n_rows // FFN_ROWS,),
        in_specs=[
            pl.BlockSpec((FFN_ROWS, d), lambda i: (i, 0)),
            _small((1, d)),
            _resident((d, 2 * D_FF)),
            _resident((D_FF, d)),
            _small((1, d)),
        ],
        out_specs=pl.BlockSpec((FFN_ROWS, d), lambda i: (i, 0)),
        out_shape=jax.ShapeDtypeStruct((n_rows, d), _F32),
        compiler_params=pltpu.CompilerParams(
            dimension_semantics=("arbitrary",), vmem_limit_bytes=VMEM_LIMIT_BYTES),
    )
    out = ffn(h_mid.reshape(n_rows, d), row(norm2_g[0]), w_ffn_in[0].astype(_BF),
              w_ffn_down[0].astype(_BF), row(norm_f_g))
    return out.reshape(nb, seq, d)
```

```python
import functools
import math

import jax
import jax.numpy as jnp
from jax import lax
from jax.experimental import pallas as pl
from jax.experimental.pallas import tpu as pltpu

D_MODEL = 1024
N_META = 16
RG_WIDTH = 1280
RG_BLOCKS = 16
RG_BLOCK = 80
RG_GROUPS = 2
RG_GROUP_W = RG_WIDTH // RG_GROUPS
RG_CONV = 4
RG_C = 8.0
HG_HEADS = 8
HG_DK = 128
HG_WIDTH = 1024
D_FF = 2816
NORM_EPS = 1e-6
LOG2_E = math.log2(math.e)

OFF_AX, OFF_AG, OFF_Q, OFF_F, OFF_I, OFF_G, OFF_GA, OFF_GB = (
    0, 1280, 2560, 3584, 4608, 5632, 6656, 7680)

SUBLANES = 8
TIME_TILE = 128
CHUNK = 64
PROJ_COLS = 512
FFN_ROWS = 512
FFN_COLS = 1408
VMEM_LIMIT_BYTES = 60000 * 1024

_BF = jnp.bfloat16
_F32 = jnp.float32


def _dot(a, b):
    return jnp.dot(a, b, preferred_element_type=_F32)


def _dot_nt(a, b):
    return lax.dot_general(a, b, (((1,), (1,)), ((), ())), preferred_element_type=_F32)


def _dot_tn(a, b):
    return lax.dot_general(a, b, (((0,), (0,)), ((), ())), preferred_element_type=_F32)


def _rmsnorm(x, g):
    ms = jnp.mean(x * x, axis=-1, keepdims=True)
    return x * lax.rsqrt(ms + NORM_EPS) * g


def _sigmoid(x):
    return 0.5 * jnp.tanh(0.5 * x) + 0.5


def _silu(x):
    hx = 0.5 * x
    return hx * jnp.tanh(hx) + hx


def _sqrt_nonneg(x):
    return jnp.where(x > 0.0, x * lax.rsqrt(x), 0.0)


def _gelu_tanh(x):
    c = math.sqrt(2.0 / math.pi)
    return x * (0.5 * (1.0 + jnp.tanh(c * (x + 0.044715 * (x * x * x)))))


def _level_masks():
    t = lax.broadcasted_iota(jnp.int32, (CHUNK, CHUNK), 0)
    s = lax.broadcasted_iota(jnp.int32, (CHUNK, CHUNK), 1)
    masks = {0: t == s}
    h = 1
    while h < CHUNK:
        sh = int(math.log2(2 * h))
        masks[h] = ((t >> sh) == (s >> sh)) & ((t & h) != 0) & ((s & h) == 0)
        h *= 2
    groups = range(0, CHUNK, SUBLANES)
    return {h: [jnp.where(m[r:r + SUBLANES], 1.0, 0.0).astype(_F32) for r in groups]
            for h, m in masks.items()}


def _hgrn_operands(g2, q, k):
    w = g2.shape[-1]
    nv = CHUNK // SUBLANES
    sub = lax.broadcasted_iota(jnp.int32, (SUBLANES, w), 0)
    qs = [q[SUBLANES * j:SUBLANES * (j + 1)] for j in range(nv)]
    ks = [k[SUBLANES * j:SUBLANES * (j + 1)] for j in range(nv)]
    bcs = []
    run = None
    for j in range(nv):
        x = g2[SUBLANES * j:SUBLANES * (j + 1)]
        for s in (1, 2, 4):
            x = x + jnp.where(sub >= s, pltpu.roll(x, s, 0), 0.0)
        if run is not None:
            x = x + run
        bcs.append(x)
        run = x[SUBLANES - 1:SUBLANES]
    b_last = run

    def small_ref_row(h, j):
        x = bcs[j]
        rows = [jnp.broadcast_to(x[r:r + 1], x.shape)
                for r in range(h - 1, SUBLANES, 2 * h)]
        out = rows[-1]
        for idx in range(len(rows) - 2, -1, -1):
            out = jnp.where(sub < 2 * h * (idx + 1), rows[idx], out)
        return out

    levels = []
    h = CHUNK // 2
    while h >= 1:
        parts = []
        for j in range(nv):
            if h >= SUBLANES:
                r = (SUBLANES * j) // (2 * h) * (2 * h) + h - 1
                ref = bcs[r // SUBLANES][SUBLANES - 1:SUBLANES]
                if (SUBLANES * j) & h:
                    parts.append(qs[j] * jnp.exp2(bcs[j] - ref))
                else:
                    parts.append(ks[j] * jnp.exp2(ref - bcs[j]))
            else:
                base = jnp.where((sub & h) != 0, qs[j], ks[j])
                parts.append(base * jnp.exp2(-jnp.abs(bcs[j] - small_ref_row(h, j))))
        levels.append((h, jnp.concatenate(parts, axis=0).astype(_BF)))
        h //= 2
    q_bf = q.astype(_BF)
    k_bf = k.astype(_BF)
    q_dec = jnp.concatenate([qs[j] * jnp.exp2(bcs[j]) for j in range(nv)],
                            axis=0).astype(_BF)
    k_dec = jnp.concatenate([ks[j] * jnp.exp2(b_last - bcs[j]) for j in range(nv)],
                            axis=0).astype(_BF)
    s_dec = jnp.exp2(b_last)
    return levels, q_bf, k_bf, q_dec, k_dec, s_dec


def _hgrn_score_products(operands, masks, tick):
    levels, q_bf, k_bf, q_dec, k_dec, s_dec = operands
    nv = CHUNK // SUBLANES
    all_scores = []
    for hd in range(HG_HEADS):
        sl = slice(hd * HG_DK, (hd + 1) * HG_DK)
        diag = _dot_nt(q_bf[:, sl], k_bf[:, sl])
        acc = [diag[SUBLANES * j:SUBLANES * (j + 1)] * masks[0][j] for j in range(nv)]
        for h, x in levels:
            prod = _dot_nt(x[:, sl], x[:, sl])
            for j in range(nv):
                if h >= SUBLANES and not (SUBLANES * j) & h:
                    continue
                acc[j] = acc[j] + prod[SUBLANES * j:SUBLANES * (j + 1)] * masks[h][j]
        all_scores.append(jnp.concatenate(acc, axis=0).astype(_BF))
        if hd % 4 == 3:
            tick()
    return all_scores, q_dec, k_dec, s_dec


def _hgrn_apply(phase1, v_bf, st):
    all_scores, q_dec, k_dec, s_dec = phase1
    outs, new_st = [], []
    for hd in range(HG_HEADS):
        sl = slice(hd * HG_DK, (hd + 1) * HG_DK)
        o = _dot(all_scores[hd], v_bf[:, sl])
        o = o + _dot_nt(q_dec[:, sl], st[hd].astype(_BF))
        new_st.append(s_dec[:, sl] * st[hd] + _dot_tn(v_bf[:, sl], k_dec[:, sl]))
        outs.append(o)
    return jnp.concatenate(outs, axis=-1), new_st


def _projection_chunks():
    plan = (("ax", OFF_AX, RG_WIDTH), ("f", OFF_F, HG_WIDTH), ("q", OFF_Q, HG_WIDTH),
            ("v", OFF_I, HG_WIDTH), ("ag", OFF_AG, RG_WIDTH), ("ga", OFF_GA, D_MODEL),
            ("g", OFF_G, HG_WIDTH), ("gb", OFF_GB, D_MODEL))
    chunks = []
    for name, off, width in plan:
        for c in range(0, width, PROJ_COLS):
            chunks.append((name, off + c, min(PROJ_COLS, width - c)))
    return chunks


def _mixer_kernel(meta_ref, x_ref, n1g_ref, w_in_ref, convw_ref, convb_ref, wg_ref,
                  ba_ref, bx_ref, lam_ref, lbl_ref, hgn_ref, wpa_ref, wpb_ref, wout_ref,
                  o_ref, cbuf, hcar, st_ref, *, nb, tt):
    i = pl.program_id(0)
    rows = nb * tt
    is_meta = i == 0

    @pl.when(is_meta)
    def _():
        cbuf[...] = jnp.zeros_like(cbuf)
        hcar[...] = jnp.zeros_like(hcar)
        st_ref[...] = jnp.zeros_like(st_ref)

    meta = meta_ref[...]
    h_in = jnp.concatenate([jnp.where(is_meta, meta, x_ref[b]) for b in range(nb)], axis=0)
    hn = _rmsnorm(h_in, n1g_ref[...]).astype(_BF)

    chunks = _projection_chunks()
    issued = {}
    cursor = iter(chunks)

    def project_next(n=1):
        for _ in range(n):
            nxt = next(cursor, None)
            if nxt is None:
                return
            name, off, width = nxt
            issued.setdefault(name, []).append(_dot(hn, w_in_ref[:, off:off + width]))

    def projected(name):
        total = sum(1 for c in chunks if c[0] == name)
        while len(issued.get(name, ())) < total:
            project_next()
        return jnp.concatenate(issued[name], axis=-1)

    a_x = projected("ax")

    convw = convw_ref[...]
    xc_parts = []
    for b in range(nb):
        cbuf[b, SUBLANES:, :] = a_x[b * tt:(b + 1) * tt]
        acc = convb_ref[...] + convw[RG_CONV - 1:RG_CONV] * cbuf[b, pl.ds(SUBLANES, tt), :]
        for j in range(RG_CONV - 1):
            off = SUBLANES - (RG_CONV - 1) + j
            acc = acc + convw[j:j + 1] * cbuf[b, pl.ds(off, tt), :]
        xc_parts.append(acc)
        cbuf[b, 0:SUBLANES, :] = cbuf[b, pl.ds(tt, SUBLANES), :]
    xc = jnp.concatenate(xc_parts, axis=0)
    xc_bf = xc.astype(_BF)
    z = -lam_ref[...]
    softplus = jnp.maximum(z, 0.0) + jnp.log1p(jnp.exp(-jnp.abs(z)))
    a_scale = (-0.5 * RG_C * LOG2_E) * softplus
    a_parts, u_parts = [], []
    for gidx in range(RG_GROUPS):
        lo = gidx * RG_GROUP_W
        cols = slice(lo, lo + RG_GROUP_W)
        zz = _dot(xc_bf[:, cols], wg_ref[gidx])
        r_tanh = jnp.tanh(0.5 * (zz[:, :RG_GROUP_W] + ba_ref[:, cols]))
        a_g = jnp.exp2(a_scale[:, cols] * r_tanh + a_scale[:, cols])
        i_g = _sigmoid(zz[:, RG_GROUP_W:] + bx_ref[:, cols])
        a_parts.append(a_g)
        u_parts.append(_sqrt_nonneg(1.0 - a_g * a_g) * (i_g * xc[:, cols]))
        project_next()
    a = jnp.concatenate(a_parts, axis=-1)
    u = jnp.concatenate(u_parts, axis=-1)
    t_loc = lax.broadcasted_iota(jnp.int32, (rows, 1), 0) % tt
    u = jnp.where(jnp.logical_or(i > 0, t_loc >= tt - N_META), u, 0.0)

    sub = lax.broadcasted_iota(jnp.int32, (SUBLANES, RG_WIDTH), 0)
    h_parts = []
    for b in range(nb):
        carry = hcar[b]
        for j in range(tt // SUBLANES):
            r0 = b * tt + j * SUBLANES
            av = a[r0:r0 + SUBLANES]
            hv = u[r0:r0 + SUBLANES]
            for s in (1, 2, 4):
                keep = sub >= s
                hv = hv + av * jnp.where(keep, pltpu.roll(hv, s, 0), 0.0)
                av = av * jnp.where(keep, pltpu.roll(av, s, 0), 1.0)
            hv = hv + av * carry
            carry = hv[SUBLANES - 1:SUBLANES]
            h_parts.append(hv)
            if j % 8 == 7:
                project_next()
        hcar[b] = carry
    h_rg = jnp.concatenate(h_parts, axis=0)

    l0 = lbl_ref[0:1, :]
    l1 = lbl_ref[1:2, :]
    lmax = jnp.maximum(l0, l1)
    e0 = jnp.exp(l0 - lmax)
    e1 = jnp.exp(l1 - lmax)
    lb = e0 / (e0 + e1)
    f_sig = _sigmoid(projected("f"))
    log2_f = jnp.log(lb + (1.0 - lb) * f_sig) * LOG2_E
    k_in = (1.0 - lb) * (1.0 - f_sig)
    q_in = _silu(projected("q"))
    project_next()
    v_in = projected("v").astype(_BF)

    masks = _level_masks()
    order = [(b, c) for c in range(tt // CHUNK) for b in range(nb)]
    st = {b: [st_ref[b, hd] for hd in range(HG_HEADS)] for b in range(nb)}
    o_chunks = {}

    def rows_of(b, c):
        r0 = b * tt + c * CHUNK
        return slice(r0, r0 + CHUNK)

    def apply_phase(b, c, phase1):
        o_chunks[(b, c)], st[b] = _hgrn_apply(phase1, v_in[rows_of(b, c)], st[b])

    prepared = []
    for b, c in order:
        sl = rows_of(b, c)
        prepared.append(_hgrn_operands(log2_f[sl], q_in[sl], k_in[sl]))
        project_next()
    pending = None
    for (b, c), operands in zip(order, prepared):
        phase1 = _hgrn_score_products(operands, masks, project_next)
        if pending is not None:
            apply_phase(*pending)
        pending = (b, c, phase1)
    apply_phase(*pending)
    for b in range(nb):
        for hd in range(HG_HEADS):
            st_ref[b, hd] = st[b][hd]
    o_b = jnp.concatenate([o_chunks[(b, c)] for b in range(nb) for c in range(tt // CHUNK)],
                          axis=0)

    y_a = (h_rg * _gelu_tanh(projected("ag"))).astype(_BF)
    merged = _sigmoid(projected("ga")) * _dot(y_a, wpa_ref[...])
    g_act = _silu(projected("g"))
    hgn = hgn_ref[...]
    yb_parts = []
    for hd in range(HG_HEADS):
        sl = slice(hd * HG_DK, (hd + 1) * HG_DK)
        yb_parts.append((_rmsnorm(o_b[:, sl], hgn) * g_act[:, sl]).astype(_BF))
    y_b = jnp.concatenate(yb_parts, axis=-1)
    merged = merged + _sigmoid(projected("gb")) * _dot(y_b, wpb_ref[...])

    out = h_in + _dot(merged.astype(_BF), wout_ref[...])
    for b in range(nb):
        o_ref[b] = out[b * tt:(b + 1) * tt]


def _ffn_kernel(h_ref, n2g_ref, wfi_ref, wfd_ref, nfg_ref, o_ref):
    h = h_ref[...]
    hn = _rmsnorm(h, n2g_ref[...]).astype(_BF)
    acc = h
    for j in range(D_FF // FFN_COLS):
        lo = j * FFN_COLS
        gcol = _dot(hn, wfi_ref[:, lo:lo + FFN_COLS])
        ucol = _dot(hn, wfi_ref[:, D_FF + lo:D_FF + lo + FFN_COLS])
        act = (_silu(gcol) * ucol).astype(_BF)
        acc = acc + _dot(act, wfd_ref[lo:lo + FFN_COLS, :])
    o_ref[...] = _rmsnorm(acc, nfg_ref[...])


def _resident(shape):
    nd = len(shape)
    return pl.BlockSpec(shape, lambda *_: (0,) * nd, pipeline_mode=pl.Buffered(1))


def _small(shape):
    nd = len(shape)
    return pl.BlockSpec(shape, lambda *_: (0,) * nd)


def _block_diag_groups(w):
    per = RG_BLOCKS // RG_GROUPS
    w = w.reshape(RG_GROUPS, per, RG_BLOCK, RG_BLOCK)
    eye = jnp.eye(per, dtype=w.dtype)
    bd = w[:, :, :, None, :] * eye[None, :, None, :, None]
    return bd.reshape(RG_GROUPS, RG_GROUP_W, RG_GROUP_W)


def kernel(x, meta_tokens, norm1_g, w_in, conv_w, conv_b, rg_wa, rg_ba, rg_wx, rg_bx,
           rg_lambda, hg_lb_logits, hg_norm_g, w_proj_a, w_proj_b, w_out, norm2_g,
           w_ffn_in, w_ffn_down, norm_f_g):
    nb, seq, d = x.shape
    assert d == D_MODEL and seq % TIME_TILE == 0 and (nb * seq) % FFN_ROWS == 0
    assert w_in.shape[0] == 1, "single-layer block"
    tt = TIME_TILE
    n_steps = 1 + seq // tt

    meta_tile = jnp.zeros((tt, d), x.dtype).at[tt - N_META:].set(meta_tokens.astype(x.dtype))
    w_in_b = w_in[0].astype(_BF)
    wg = jnp.concatenate([_block_diag_groups(rg_wa[0]), _block_diag_groups(rg_wx[0])],
                         axis=-1).astype(_BF)
    row = lambda v: v.reshape(1, -1).astype(_F32)

    x_map = lambda i: (0, jnp.maximum(i - 1, 0), 0)
    mixer = pl.pallas_call(
        functools.partial(_mixer_kernel, nb=nb, tt=tt),
        name="mixer",
        grid=(n_steps,),
        in_specs=[
            _resident((tt, d)),
            pl.BlockSpec((nb, tt, d), x_map),
            _small((1, d)),
            _resident(w_in_b.shape),
            _small((RG_CONV, RG_WIDTH)),
            _small((1, RG_WIDTH)),
            _resident(wg.shape),
            _small((1, RG_WIDTH)),
            _small((1, RG_WIDTH)),
            _small((1, RG_WIDTH)),
            _small((2, HG_WIDTH)),
            _small((1, HG_DK)),
            _resident((RG_WIDTH, d)),
            _resident((HG_WIDTH, d)),
            _resident((d, d)),
        ],
        out_specs=pl.BlockSpec((nb, tt, d), x_map),
        out_shape=jax.ShapeDtypeStruct((nb, seq, d), _F32),
        scratch_shapes=[
            pltpu.VMEM((nb, tt + SUBLANES, RG_WIDTH), _F32),
            pltpu.VMEM((nb, 1, RG_WIDTH), _F32),
            pltpu.VMEM((nb, HG_HEADS, HG_DK, HG_DK), _F32),
        ],
        compiler_params=pltpu.CompilerParams(
            dimension_semantics=("arbitrary",), vmem_limit_bytes=VMEM_LIMIT_BYTES),
    )
    h_mid = mixer(
        meta_tile, x, row(norm1_g[0]), w_in_b, conv_w[0].astype(_F32), row(conv_b[0]), wg,
        row(rg_ba[0]), row(rg_bx[0]), row(rg_lambda[0]), hg_lb_logits.astype(_F32),
        row(hg_norm_g[0]), w_proj_a[0].astype(_BF), w_proj_b[0].astype(_BF),
        w_out[0].astype(_BF))

    n_rows = nb * seq
    ffn = pl.pallas_call(
        _ffn_kernel,
        name="ffn",
        grid=(n_rows // FFN_ROWS,),
        in_specs=[
            pl.BlockSpec((FFN_ROWS, d), lambda i: (i, 0)),
            _small((1, d)),
            _resident((d, 2 * D_FF)),
            _resident((D_FF, d)),
            _small((1, d)),
        ],
        out_specs=pl.BlockSpec((FFN_ROWS, d), lambda i: (i, 0)),
        out_shape=jax.ShapeDtypeStruct((n_rows, d), _F32),
        compiler_params=pltpu.CompilerParams(
            dimension_semantics=("arbitrary",), vmem_limit_bytes=VMEM_LIMIT_BYTES),
    )
    out = ffn(h_mid.reshape(n_rows, d), row(norm2_g[0]), w_ffn_in[0].astype(_BF),
              w_ffn_down[0].astype(_BF), row(norm_f_g))
    return out.reshape(nb, seq, d)
```

```python
import functools
import math

import jax
import jax.numpy as jnp
from jax import lax
from jax.experimental import pallas as pl
from jax.experimental.pallas import tpu as pltpu

D_MODEL = 1024
N_META = 16
RG_WIDTH = 1280
RG_BLOCKS = 16
RG_BLOCK = 80
RG_GROUPS = 2
RG_GROUP_W = RG_WIDTH // RG_GROUPS
RG_CONV = 4
RG_C = 8.0
HG_HEADS = 8
HG_DK = 128
HG_WIDTH = 1024
D_FF = 2816
NORM_EPS = 1e-6
LOG2_E = math.log2(math.e)

OFF_AX, OFF_AG, OFF_Q, OFF_F, OFF_I, OFF_G, OFF_GA, OFF_GB = (
    0, 1280, 2560, 3584, 4608, 5632, 6656, 7680)

SUBLANES = 8
TIME_TILE = 128
CHUNK = 64
PROJ_COLS = 512
FFN_ROWS = 512
FFN_COLS = 1408
VMEM_LIMIT_BYTES = 60000 * 1024

_BF = jnp.bfloat16
_F32 = jnp.float32


def _dot(a, b):
    return jnp.dot(a, b, preferred_element_type=_F32)


def _dot_nt(a, b):
    return lax.dot_general(a, b, (((1,), (1,)), ((), ())), preferred_element_type=_F32)


def _dot_tn(a, b):
    return lax.dot_general(a, b, (((0,), (0,)), ((), ())), preferred_element_type=_F32)


def _rmsnorm(x, g):
    ms = jnp.mean(x * x, axis=-1, keepdims=True)
    return x * lax.rsqrt(ms + NORM_EPS) * g


def _sigmoid(x):
    return 0.5 * jnp.tanh(0.5 * x) + 0.5


def _silu(x):
    hx = 0.5 * x
    return hx * jnp.tanh(hx) + hx


def _sqrt_nonneg(x):
    return jnp.where(x > 0.0, x * lax.rsqrt(x), 0.0)


def _gelu_tanh(x):
    c = math.sqrt(2.0 / math.pi)
    return x * (0.5 * (1.0 + jnp.tanh(c * (x + 0.044715 * (x * x * x)))))


def _level_masks():
    t = lax.broadcasted_iota(jnp.int32, (CHUNK, CHUNK), 0)
    s = lax.broadcasted_iota(jnp.int32, (CHUNK, CHUNK), 1)
    masks = {0: t == s}
    h = 1
    while h < CHUNK:
        sh = int(math.log2(2 * h))
        masks[h] = ((t >> sh) == (s >> sh)) & ((t & h) != 0) & ((s & h) == 0)
        h *= 2
    groups = range(0, CHUNK, SUBLANES)
    return {h: [jnp.where(m[r:r + SUBLANES], 1.0, 0.0).astype(_F32) for r in groups]
            for h, m in masks.items()}


def _hgrn_operands(g2, q, k):
    w = g2.shape[-1]
    nv = CHUNK // SUBLANES
    sub = lax.broadcasted_iota(jnp.int32, (SUBLANES, w), 0)
    qs = [q[SUBLANES * j:SUBLANES * (j + 1)] for j in range(nv)]
    ks = [k[SUBLANES * j:SUBLANES * (j + 1)] for j in range(nv)]
    bcs = []
    run = None
    for j in range(nv):
        x = g2[SUBLANES * j:SUBLANES * (j + 1)]
        for s in (1, 2, 4):
            x = x + jnp.where(sub >= s, pltpu.roll(x, s, 0), 0.0)
        if run is not None:
            x = x + run
        bcs.append(x)
        run = x[SUBLANES - 1:SUBLANES]
    b_last = run

    def small_ref_row(h, j):
        x = bcs[j]
        rows = [jnp.broadcast_to(x[r:r + 1], x.shape)
                for r in range(h - 1, SUBLANES, 2 * h)]
        out = rows[-1]
        for idx in range(len(rows) - 2, -1, -1):
            out = jnp.where(sub < 2 * h * (idx + 1), rows[idx], out)
        return out

    levels = []
    h = CHUNK // 2
    while h >= 1:
        parts = []
        for j in range(nv):
            if h >= SUBLANES:
                r = (SUBLANES * j) // (2 * h) * (2 * h) + h - 1
                ref = bcs[r // SUBLANES][SUBLANES - 1:SUBLANES]
                if (SUBLANES * j) & h:
                    parts.append(qs[j] * jnp.exp2(bcs[j] - ref))
                else:
                    parts.append(ks[j] * jnp.exp2(ref - bcs[j]))
            else:
                base = jnp.where((sub & h) != 0, qs[j], ks[j])
                parts.append(base * jnp.exp2(-jnp.abs(bcs[j] - small_ref_row(h, j))))
        levels.append((h, jnp.concatenate(parts, axis=0).astype(_BF)))
        h //= 2
    q_bf = q.astype(_BF)
    k_bf = k.astype(_BF)
    q_dec = jnp.concatenate([qs[j] * jnp.exp2(bcs[j]) for j in range(nv)],
                            axis=0).astype(_BF)
    k_dec = jnp.concatenate([ks[j] * jnp.exp2(b_last - bcs[j]) for j in range(nv)],
                            axis=0).astype(_BF)
    s_dec = jnp.exp2(b_last)
    return levels, q_bf, k_bf, q_dec, k_dec, s_dec


def _hgrn_score_products(operands, masks, tick):
    levels, q_bf, k_bf, q_dec, k_dec, s_dec = operands
    nv = CHUNK // SUBLANES
    all_scores = []
    for hd in range(HG_HEADS):
        sl = slice(hd * HG_DK, (hd + 1) * HG_DK)
        diag = _dot_nt(q_bf[:, sl], k_bf[:, sl])
        acc = [diag[SUBLANES * j:SUBLANES * (j + 1)] * masks[0][j] for j in range(nv)]
        for h, x in levels:
            prod = _dot_nt(x[:, sl], x[:, sl])
            for j in range(nv):
                if h >= SUBLANES and not (SUBLANES * j) & h:
                    continue
                acc[j] = acc[j] + prod[SUBLANES * j:SUBLANES * (j + 1)] * masks[h][j]
        all_scores.append(jnp.concatenate(acc, axis=0).astype(_BF))
        if hd % 4 == 3:
            tick()
    return all_scores, q_dec, k_dec, s_dec


def _hgrn_apply(phase1, v_bf, st):
    all_scores, q_dec, k_dec, s_dec = phase1
    outs, new_st = [], []
    for hd in range(HG_HEADS):
        sl = slice(hd * HG_DK, (hd + 1) * HG_DK)
        o = _dot(all_scores[hd], v_bf[:, sl])
        o = o + _dot_nt(q_dec[:, sl], st[hd].astype(_BF))
        new_st.append(s_dec[:, sl] * st[hd] + _dot_tn(v_bf[:, sl], k_dec[:, sl]))
        outs.append(o)
    return jnp.concatenate(outs, axis=-1), new_st


def _projection_chunks():
    plan = (("ax", OFF_AX, RG_WIDTH), ("f", OFF_F, HG_WIDTH), ("q", OFF_Q, HG_WIDTH),
            ("v", OFF_I, HG_WIDTH), ("ag", OFF_AG, RG_WIDTH), ("ga", OFF_GA, D_MODEL),
            ("g", OFF_G, HG_WIDTH), ("gb", OFF_GB, D_MODEL))
    chunks = []
    for name, off, width in plan:
        for c in range(0, width, PROJ_COLS):
            chunks.append((name, off + c, min(PROJ_COLS, width - c)))
    return chunks


def _mixer_kernel(meta_ref, x_ref, n1g_ref, w_in_ref, convw_ref, convb_ref, wg_ref,
                  ba_ref, bx_ref, lam_ref, lbl_ref, hgn_ref, wpa_ref, wpb_ref, wout_ref,
                  o_ref, cbuf, hcar, st_ref, *, nb, tt):
    i = pl.program_id(0)
    rows = nb * tt
    is_meta = i == 0

    @pl.when(is_meta)
    def _():
        cbuf[...] = jnp.zeros_like(cbuf)
        hcar[...] = jnp.zeros_like(hcar)
        st_ref[...] = jnp.zeros_like(st_ref)

    meta = meta_ref[...]
    h_in = jnp.concatenate([jnp.where(is_meta, meta, x_ref[b]) for b in range(nb)], axis=0)
    hn = _rmsnorm(h_in, n1g_ref[...]).astype(_BF)

    chunks = _projection_chunks()
    issued = {}
    cursor = iter(chunks)

    def project_next(n=1):
        for _ in range(n):
            nxt = next(cursor, None)
            if nxt is None:
                return
            name, off, width = nxt
            issued.setdefault(name, []).append(_dot(hn, w_in_ref[:, off:off + width]))

    def projected(name):
        total = sum(1 for c in chunks if c[0] == name)
        while len(issued.get(name, ())) < total:
            project_next()
        return jnp.concatenate(issued[name], axis=-1)

    a_x = projected("ax")

    convw = convw_ref[...]
    xc_parts = []
    for b in range(nb):
        cbuf[b, SUBLANES:, :] = a_x[b * tt:(b + 1) * tt]
        acc = convb_ref[...] + convw[RG_CONV - 1:RG_CONV] * cbuf[b, pl.ds(SUBLANES, tt), :]
        for j in range(RG_CONV - 1):
            off = SUBLANES - (RG_CONV - 1) + j
            acc = acc + convw[j:j + 1] * cbuf[b, pl.ds(off, tt), :]
        xc_parts.append(acc)
        cbuf[b, 0:SUBLANES, :] = cbuf[b, pl.ds(tt, SUBLANES), :]
    xc = jnp.concatenate(xc_parts, axis=0)
    xc_bf = xc.astype(_BF)
    z = -lam_ref[...]
    softplus = jnp.maximum(z, 0.0) + jnp.log1p(jnp.exp(-jnp.abs(z)))
    a_scale = (-0.5 * RG_C * LOG2_E) * softplus
    a_parts, u_parts = [], []
    for gidx in range(RG_GROUPS):
        lo = gidx * RG_GROUP_W
        cols = slice(lo, lo + RG_GROUP_W)
        zz = _dot(xc_bf[:, cols], wg_ref[gidx])
        r_tanh = jnp.tanh(0.5 * (zz[:, :RG_GROUP_W] + ba_ref[:, cols]))
        a_g = jnp.exp2(a_scale[:, cols] * r_tanh + a_scale[:, cols])
        i_g = _sigmoid(zz[:, RG_GROUP_W:] + bx_ref[:, cols])
        a_parts.append(a_g)
        u_parts.append(_sqrt_nonneg(1.0 - a_g * a_g) * (i_g * xc[:, cols]))
        project_next()
    a = jnp.concatenate(a_parts, axis=-1)
    u = jnp.concatenate(u_parts, axis=-1)
    t_loc = lax.broadcasted_iota(jnp.int32, (rows, 1), 0) % tt
    u = jnp.where(jnp.logical_or(i > 0, t_loc >= tt - N_META), u, 0.0)

    sub = lax.broadcasted_iota(jnp.int32, (SUBLANES, RG_WIDTH), 0)
    h_parts = []
    for b in range(nb):
        carry = hcar[b]
        for j in range(tt // SUBLANES):
            r0 = b * tt + j * SUBLANES
            av = a[r0:r0 + SUBLANES]
            hv = u[r0:r0 + SUBLANES]
            for s in (1, 2, 4):
                keep = sub >= s
                hv = hv + av * jnp.where(keep, pltpu.roll(hv, s, 0), 0.0)
                av = av * jnp.where(keep, pltpu.roll(av, s, 0), 1.0)
            hv = hv + av * carry
            carry = hv[SUBLANES - 1:SUBLANES]
            h_parts.append(hv)
            if j % 8 == 7:
                project_next()
        hcar[b] = carry
    h_rg = jnp.concatenate(h_parts, axis=0)

    l0 = lbl_ref[0:1, :]
    l1 = lbl_ref[1:2, :]
    lmax = jnp.maximum(l0, l1)
    e0 = jnp.exp(l0 - lmax)
    e1 = jnp.exp(l1 - lmax)
    lb = e0 / (e0 + e1)
    f_sig = _sigmoid(projected("f"))
    log2_f = jnp.log(lb + (1.0 - lb) * f_sig) * LOG2_E
    k_in = (1.0 - lb) * (1.0 - f_sig)
    q_in = _silu(projected("q"))
    project_next()
    v_in = projected("v").astype(_BF)

    masks = _level_masks()
    order = [(b, c) for c in range(tt // CHUNK) for b in range(nb)]
    st = {b: [st_ref[b, hd] for hd in range(HG_HEADS)] for b in range(nb)}
    o_chunks = {}

    def rows_of(b, c):
        r0 = b * tt + c * CHUNK
        return slice(r0, r0 + CHUNK)

    def apply_phase(b, c, phase1):
        o_chunks[(b, c)], st[b] = _hgrn_apply(phase1, v_in[rows_of(b, c)], st[b])

    prepared = []
    for b, c in order:
        sl = rows_of(b, c)
        prepared.append(_hgrn_operands(log2_f[sl], q_in[sl], k_in[sl]))
        project_next()
    pending = None
    for (b, c), operands in zip(order, prepared):
        phase1 = _hgrn_score_products(operands, masks, project_next)
        if pending is not None:
            apply_phase(*pending)
        pending = (b, c, phase1)
    apply_phase(*pending)
    for b in range(nb):
        for hd in range(HG_HEADS):
            st_ref[b, hd] = st[b][hd]
    o_b = jnp.concatenate([o_chunks[(b, c)] for b in range(nb) for c in range(tt // CHUNK)],
                          axis=0)

    y_a = (h_rg * _gelu_tanh(projected("ag"))).astype(_BF)
    merged = _sigmoid(projected("ga")) * _dot(y_a, wpa_ref[...])
    g_act = _silu(projected("g"))
    hgn = hgn_ref[...]
    yb_parts = []
    for hd in range(HG_HEADS):
        sl = slice(hd * HG_DK, (hd + 1) * HG_DK)
        yb_parts.append((_rmsnorm(o_b[:, sl], hgn) * g_act[:, sl]).astype(_BF))
    y_b = jnp.concatenate(yb_parts, axis=-1)
    merged = merged + _sigmoid(projected("gb")) * _dot(y_b, wpb_ref[...])

    out = h_in + _dot(merged.astype(_BF), wout_ref[...])
    for b in range(nb):
        o_ref[b] = out[b * tt:(b + 1) * tt]


def _ffn_kernel(h_ref, n2g_ref, wfi_ref, wfd_ref, nfg_ref, o_ref):
    h = h_ref[...]
    hn = _rmsnorm(h, n2g_ref[...]).astype(_BF)
    acc = h
    for j in range(D_FF // FFN_COLS):
        lo = j * FFN_COLS
        gcol = _dot(hn, wfi_ref[:, lo:lo + FFN_COLS].astype(_BF))
        ucol = _dot(hn, wfi_ref[:, D_FF + lo:D_FF + lo + FFN_COLS].astype(_BF))
        act = (_silu(gcol) * ucol).astype(_BF)
        acc = acc + _dot(act, wfd_ref[lo:lo + FFN_COLS, :].astype(_BF))
    o_ref[...] = _rmsnorm(acc, nfg_ref[...])


def _resident(shape):
    nd = len(shape)
    return pl.BlockSpec(shape, lambda *_: (0,) * nd, pipeline_mode=pl.Buffered(1))


def _small(shape):
    nd = len(shape)
    return pl.BlockSpec(shape, lambda *_: (0,) * nd)


def _block_diag_groups(w):
    per = RG_BLOCKS // RG_GROUPS
    w = w.reshape(RG_GROUPS, per, RG_BLOCK, RG_BLOCK)
    eye = jnp.eye(per, dtype=w.dtype)
    bd = w[:, :, :, None, :] * eye[None, :, None, :, None]
    return bd.reshape(RG_GROUPS, RG_GROUP_W, RG_GROUP_W)


def kernel(x, meta_tokens, norm1_g, w_in, conv_w, conv_b, rg_wa, rg_ba, rg_wx, rg_bx,
           rg_lambda, hg_lb_logits, hg_norm_g, w_proj_a, w_proj_b, w_out, norm2_g,
           w_ffn_in, w_ffn_down, norm_f_g):
    nb, seq, d = x.shape
    assert d == D_MODEL and seq % TIME_TILE == 0 and (nb * seq) % FFN_ROWS == 0
    assert w_in.shape[0] == 1, "single-layer block"
    tt = TIME_TILE
    n_steps = 1 + seq // tt

    meta_tile = jnp.zeros((tt, d), x.dtype).at[tt - N_META:].set(meta_tokens.astype(x.dtype))
    w_in_b = w_in[0].astype(_BF)
    wg = jnp.concatenate([_block_diag_groups(rg_wa[0]), _block_diag_groups(rg_wx[0])],
                         axis=-1).astype(_BF)
    row = lambda v: v.reshape(1, -1).astype(_F32)

    x_map = lambda i: (0, jnp.maximum(i - 1, 0), 0)
    mixer = pl.pallas_call(
        functools.partial(_mixer_kernel, nb=nb, tt=tt),
        name="mixer",
        grid=(n_steps,),
        in_specs=[
            _resident((tt, d)),
            pl.BlockSpec((nb, tt, d), x_map),
            _small((1, d)),
            _resident(w_in_b.shape),
            _small((RG_CONV, RG_WIDTH)),
            _small((1, RG_WIDTH)),
            _resident(wg.shape),
            _small((1, RG_WIDTH)),
            _small((1, RG_WIDTH)),
            _small((1, RG_WIDTH)),
            _small((2, HG_WIDTH)),
            _small((1, HG_DK)),
            _resident((RG_WIDTH, d)),
            _resident((HG_WIDTH, d)),
            _resident((d, d)),
        ],
        out_specs=pl.BlockSpec((nb, tt, d), x_map),
        out_shape=jax.ShapeDtypeStruct((nb, seq, d), _F32),
        scratch_shapes=[
            pltpu.VMEM((nb, tt + SUBLANES, RG_WIDTH), _F32),
            pltpu.VMEM((nb, 1, RG_WIDTH), _F32),
            pltpu.VMEM((nb, HG_HEADS, HG_DK, HG_DK), _F32),
        ],
        compiler_params=pltpu.CompilerParams(
            dimension_semantics=("arbitrary",), vmem_limit_bytes=VMEM_LIMIT_BYTES),
    )
    h_mid = mixer(
        meta_tile, x, row(norm1_g[0]), w_in_b, conv_w[0].astype(_F32), row(conv_b[0]), wg,
        row(rg_ba[0]), row(rg_bx[0]), row(rg_lambda[0]), hg_lb_logits.astype(_F32),
        row(hg_norm_g[0]), w_proj_a[0].astype(_BF), w_proj_b[0].astype(_BF),
        w_out[0].astype(_BF))

    n_rows = nb * seq
    ffn = pl.pallas_call(
        _ffn_kernel,
        name="ffn",
        grid=(n_rows // FFN_ROWS,),
        in_specs=[
            pl.BlockSpec((FFN_ROWS, d), lambda i: (i, 0)),
            _small((1, d)),
            _resident((d, 2 * D_FF)),
            _resident((D_FF, d)),
            _small((1, d)),
        ],
        out_specs=pl.BlockSpec((FFN_ROWS, d), lambda i: (i, 0)),
        out_shape=jax.ShapeDtypeStruct((n_rows, d), _F32),
        compiler_params=pltpu.CompilerParams(
            dimension_semantics=("arbitrary",), vmem_limit_bytes=VMEM_LIMIT_BYTES),
    )
    out = ffn(h_mid.reshape(n_rows, d), row(norm2_g[0]), w_ffn_in[0].astype(_F32),
              w_ffn_down[0].astype(_F32), row(norm_f_g))
    return out.reshape(nb, seq, d)
```

```python
import functools
import math

import jax
import jax.numpy as jnp
from jax import lax
from jax.experimental import pallas as pl
from jax.experimental.pallas import tpu as pltpu

D_MODEL = 1024
N_META = 16
RG_WIDTH = 1280
RG_BLOCKS = 16
RG_BLOCK = 80
RG_GROUPS = 2
RG_GROUP_W = RG_WIDTH // RG_GROUPS
RG_CONV = 4
RG_C = 8.0
HG_HEADS = 8
HG_DK = 128
HG_WIDTH = 1024
D_FF = 2816
NORM_EPS = 1e-6
LOG2_E = math.log2(math.e)

OFF_AX, OFF_AG, OFF_Q, OFF_F, OFF_I, OFF_G, OFF_GA, OFF_GB = (
    0, 1280, 2560, 3584, 4608, 5632, 6656, 7680)

SUBLANES = 8
TIME_TILE = 128
CHUNK = 128
PROJ_COLS = 512
FFN_ROWS = 512
FFN_COLS = 1408
VMEM_LIMIT_BYTES = 60000 * 1024

_BF = jnp.bfloat16
_F32 = jnp.float32


def _dot(a, b):
    return jnp.dot(a, b, preferred_element_type=_F32)


def _dot_nt(a, b):
    return lax.dot_general(a, b, (((1,), (1,)), ((), ())), preferred_element_type=_F32)


def _dot_tn(a, b):
    return lax.dot_general(a, b, (((0,), (0,)), ((), ())), preferred_element_type=_F32)


def _rmsnorm(x, g):
    ms = jnp.mean(x * x, axis=-1, keepdims=True)
    return x * lax.rsqrt(ms + NORM_EPS) * g


def _sigmoid(x):
    return 0.5 * jnp.tanh(0.5 * x) + 0.5


def _silu(x):
    hx = 0.5 * x
    return hx * jnp.tanh(hx) + hx


def _sqrt_nonneg(x):
    return jnp.where(x > 0.0, x * lax.rsqrt(x), 0.0)


def _gelu_tanh(x):
    c = math.sqrt(2.0 / math.pi)
    return x * (0.5 * (1.0 + jnp.tanh(c * (x + 0.044715 * (x * x * x)))))


def _level_masks():
    t = lax.broadcasted_iota(jnp.int32, (CHUNK, CHUNK), 0)
    s = lax.broadcasted_iota(jnp.int32, (CHUNK, CHUNK), 1)
    masks = {0: t == s}
    h = 1
    while h < CHUNK:
        sh = int(math.log2(2 * h))
        masks[h] = ((t >> sh) == (s >> sh)) & ((t & h) != 0) & ((s & h) == 0)
        h *= 2
    groups = range(0, CHUNK, SUBLANES)
    return {h: [jnp.where(m[r:r + SUBLANES], 1.0, 0.0).astype(_F32) for r in groups]
            for h, m in masks.items()}


def _hgrn_operands(g2, q, k):
    w = g2.shape[-1]
    nv = CHUNK // SUBLANES
    sub = lax.broadcasted_iota(jnp.int32, (SUBLANES, w), 0)
    qs = [q[SUBLANES * j:SUBLANES * (j + 1)] for j in range(nv)]
    ks = [k[SUBLANES * j:SUBLANES * (j + 1)] for j in range(nv)]
    bcs = []
    run = None
    for j in range(nv):
        x = g2[SUBLANES * j:SUBLANES * (j + 1)]
        for s in (1, 2, 4):
            x = x + jnp.where(sub >= s, pltpu.roll(x, s, 0), 0.0)
        if run is not None:
            x = x + run
        bcs.append(x)
        run = x[SUBLANES - 1:SUBLANES]
    b_last = run

    def small_ref_row(h, j):
        x = bcs[j]
        rows = [jnp.broadcast_to(x[r:r + 1], x.shape)
                for r in range(h - 1, SUBLANES, 2 * h)]
        out = rows[-1]
        for idx in range(len(rows) - 2, -1, -1):
            out = jnp.where(sub < 2 * h * (idx + 1), rows[idx], out)
        return out

    levels = []
    h = CHUNK // 2
    while h >= 1:
        parts = []
        for j in range(nv):
            if h >= SUBLANES:
                r = (SUBLANES * j) // (2 * h) * (2 * h) + h - 1
                ref = bcs[r // SUBLANES][SUBLANES - 1:SUBLANES]
                if (SUBLANES * j) & h:
                    parts.append(qs[j] * jnp.exp2(bcs[j] - ref))
                else:
                    parts.append(ks[j] * jnp.exp2(ref - bcs[j]))
            else:
                base = jnp.where((sub & h) != 0, qs[j], ks[j])
                parts.append(base * jnp.exp2(-jnp.abs(bcs[j] - small_ref_row(h, j))))
        levels.append((h, jnp.concatenate(parts, axis=0).astype(_BF)))
        h //= 2
    q_bf = q.astype(_BF)
    k_bf = k.astype(_BF)
    q_dec = jnp.concatenate([qs[j] * jnp.exp2(bcs[j]) for j in range(nv)],
                            axis=0).astype(_BF)
    k_dec = jnp.concatenate([ks[j] * jnp.exp2(b_last - bcs[j]) for j in range(nv)],
                            axis=0).astype(_BF)
    s_dec = jnp.exp2(b_last)
    return levels, q_bf, k_bf, q_dec, k_dec, s_dec


def _hgrn_score_products(operands, masks, tick):
    levels, q_bf, k_bf, q_dec, k_dec, s_dec = operands
    nv = CHUNK // SUBLANES
    all_scores = []
    for hd in range(HG_HEADS):
        sl = slice(hd * HG_DK, (hd + 1) * HG_DK)
        diag = _dot_nt(q_bf[:, sl], k_bf[:, sl])
        acc = [diag[SUBLANES * j:SUBLANES * (j + 1)] * masks[0][j] for j in range(nv)]
        for h, x in levels:
            prod = _dot_nt(x[:, sl], x[:, sl])
            for j in range(nv):
                if h >= SUBLANES and not (SUBLANES * j) & h:
                    continue
                acc[j] = acc[j] + prod[SUBLANES * j:SUBLANES * (j + 1)] * masks[h][j]
        all_scores.append(jnp.concatenate(acc, axis=0).astype(_BF))
        if hd % 4 == 3:
            tick()
    return all_scores, q_dec, k_dec, s_dec


def _hgrn_apply(phase1, v_bf, st):
    all_scores, q_dec, k_dec, s_dec = phase1
    outs, new_st = [], []
    for hd in range(HG_HEADS):
        sl = slice(hd * HG_DK, (hd + 1) * HG_DK)
        o = _dot(all_scores[hd], v_bf[:, sl])
        o = o + _dot_nt(q_dec[:, sl], st[hd].astype(_BF))
        new_st.append(s_dec[:, sl] * st[hd] + _dot_tn(v_bf[:, sl], k_dec[:, sl]))
        outs.append(o)
    return jnp.concatenate(outs, axis=-1), new_st


def _projection_chunks():
    plan = (("ax", OFF_AX, RG_WIDTH), ("f", OFF_F, HG_WIDTH), ("q", OFF_Q, HG_WIDTH),
            ("v", OFF_I, HG_WIDTH), ("ag", OFF_AG, RG_WIDTH), ("ga", OFF_GA, D_MODEL),
            ("g", OFF_G, HG_WIDTH), ("gb", OFF_GB, D_MODEL))
    chunks = []
    for name, off, width in plan:
        for c in range(0, width, PROJ_COLS):
            chunks.append((name, off + c, min(PROJ_COLS, width - c)))
    return chunks


def _mixer_kernel(meta_ref, x_ref, n1g_ref, w_in_ref, convw_ref, convb_ref, wg_ref,
                  ba_ref, bx_ref, lam_ref, lbl_ref, hgn_ref, wpa_ref, wpb_ref, wout_ref,
                  o_ref, cbuf, hcar, st_ref, *, nb, tt):
    i = pl.program_id(0)
    rows = nb * tt
    is_meta = i == 0

    @pl.when(is_meta)
    def _():
        cbuf[...] = jnp.zeros_like(cbuf)
        hcar[...] = jnp.zeros_like(hcar)
        st_ref[...] = jnp.zeros_like(st_ref)

    meta = meta_ref[...]
    h_in = jnp.concatenate([jnp.where(is_meta, meta, x_ref[b]) for b in range(nb)], axis=0)
    hn = _rmsnorm(h_in, n1g_ref[...]).astype(_BF)

    chunks = _projection_chunks()
    issued = {}
    cursor = iter(chunks)

    def project_next(n=1):
        for _ in range(n):
            nxt = next(cursor, None)
            if nxt is None:
                return
            name, off, width = nxt
            issued.setdefault(name, []).append(_dot(hn, w_in_ref[:, off:off + width]))

    def projected(name):
        total = sum(1 for c in chunks if c[0] == name)
        while len(issued.get(name, ())) < total:
            project_next()
        return jnp.concatenate(issued[name], axis=-1)

    a_x = projected("ax")

    convw = convw_ref[...]
    xc_parts = []
    for b in range(nb):
        cbuf[b, SUBLANES:, :] = a_x[b * tt:(b + 1) * tt]
        acc = convb_ref[...] + convw[RG_CONV - 1:RG_CONV] * cbuf[b, pl.ds(SUBLANES, tt), :]
        for j in range(RG_CONV - 1):
            off = SUBLANES - (RG_CONV - 1) + j
            acc = acc + convw[j:j + 1] * cbuf[b, pl.ds(off, tt), :]
        xc_parts.append(acc)
        cbuf[b, 0:SUBLANES, :] = cbuf[b, pl.ds(tt, SUBLANES), :]
    xc = jnp.concatenate(xc_parts, axis=0)
    xc_bf = xc.astype(_BF)
    z = -lam_ref[...]
    softplus = jnp.maximum(z, 0.0) + jnp.log1p(jnp.exp(-jnp.abs(z)))
    a_scale = (-0.5 * RG_C * LOG2_E) * softplus
    a_parts, u_parts = [], []
    for gidx in range(RG_GROUPS):
        lo = gidx * RG_GROUP_W
        cols = slice(lo, lo + RG_GROUP_W)
        zz = _dot(xc_bf[:, cols], wg_ref[gidx])
        r_tanh = jnp.tanh(0.5 * (zz[:, :RG_GROUP_W] + ba_ref[:, cols]))
        a_g = jnp.exp2(a_scale[:, cols] * r_tanh + a_scale[:, cols])
        i_g = _sigmoid(zz[:, RG_GROUP_W:] + bx_ref[:, cols])
        a_parts.append(a_g)
        u_parts.append(_sqrt_nonneg(1.0 - a_g * a_g) * (i_g * xc[:, cols]))
        project_next()
    a = jnp.concatenate(a_parts, axis=-1)
    u = jnp.concatenate(u_parts, axis=-1)
    t_loc = lax.broadcasted_iota(jnp.int32, (rows, 1), 0) % tt
    u = jnp.where(jnp.logical_or(i > 0, t_loc >= tt - N_META), u, 0.0)

    sub = lax.broadcasted_iota(jnp.int32, (SUBLANES, RG_WIDTH), 0)
    h_parts = []
    for b in range(nb):
        carry = hcar[b]
        for j in range(tt // SUBLANES):
            r0 = b * tt + j * SUBLANES
            av = a[r0:r0 + SUBLANES]
            hv = u[r0:r0 + SUBLANES]
            for s in (1, 2, 4):
                keep = sub >= s
                hv = hv + av * jnp.where(keep, pltpu.roll(hv, s, 0), 0.0)
                av = av * jnp.where(keep, pltpu.roll(av, s, 0), 1.0)
            hv = hv + av * carry
            carry = hv[SUBLANES - 1:SUBLANES]
            h_parts.append(hv)
            if j % 8 == 7:
                project_next()
        hcar[b] = carry
    h_rg = jnp.concatenate(h_parts, axis=0)

    l0 = lbl_ref[0:1, :]
    l1 = lbl_ref[1:2, :]
    lmax = jnp.maximum(l0, l1)
    e0 = jnp.exp(l0 - lmax)
    e1 = jnp.exp(l1 - lmax)
    lb = e0 / (e0 + e1)
    f_sig = _sigmoid(projected("f"))
    log2_f = jnp.log(lb + (1.0 - lb) * f_sig) * LOG2_E
    k_in = (1.0 - lb) * (1.0 - f_sig)
    q_in = _silu(projected("q"))
    project_next()
    v_in = projected("v").astype(_BF)

    masks = _level_masks()
    order = [(b, c) for c in range(tt // CHUNK) for b in range(nb)]
    st = {b: [st_ref[b, hd] for hd in range(HG_HEADS)] for b in range(nb)}
    o_chunks = {}

    def rows_of(b, c):
        r0 = b * tt + c * CHUNK
        return slice(r0, r0 + CHUNK)

    def apply_phase(b, c, phase1):
        o_chunks[(b, c)], st[b] = _hgrn_apply(phase1, v_in[rows_of(b, c)], st[b])

    prepared = []
    for b, c in order:
        sl = rows_of(b, c)
        prepared.append(_hgrn_operands(log2_f[sl], q_in[sl], k_in[sl]))
        project_next()
    pending = None
    for (b, c), operands in zip(order, prepared):
        phase1 = _hgrn_score_products(operands, masks, project_next)
        if pending is not None:
            apply_phase(*pending)
        pending = (b, c, phase1)
    apply_phase(*pending)
    for b in range(nb):
        for hd in range(HG_HEADS):
            st_ref[b, hd] = st[b][hd]
    o_b = jnp.concatenate([o_chunks[(b, c)] for b in range(nb) for c in range(tt // CHUNK)],
                          axis=0)

    y_a = (h_rg * _gelu_tanh(projected("ag"))).astype(_BF)
    merged = _sigmoid(projected("ga")) * _dot(y_a, wpa_ref[...])
    g_act = _silu(projected("g"))
    hgn = hgn_ref[...]
    yb_parts = []
    for hd in range(HG_HEADS):
        sl = slice(hd * HG_DK, (hd + 1) * HG_DK)
        yb_parts.append((_rmsnorm(o_b[:, sl], hgn) * g_act[:, sl]).astype(_BF))
    y_b = jnp.concatenate(yb_parts, axis=-1)
    merged = merged + _sigmoid(projected("gb")) * _dot(y_b, wpb_ref[...])

    out = h_in + _dot(merged.astype(_BF), wout_ref[...])
    for b in range(nb):
        o_ref[b] = out[b * tt:(b + 1) * tt]


def _ffn_kernel(h_ref, n2g_ref, wfi_ref, wfd_ref, nfg_ref, o_ref):
    half = FFN_ROWS // 2
    rows = [slice(0, half), slice(half, FFN_ROWS)]
    accs = [h_ref[r, :] for r in rows]
    hns = [_rmsnorm(a, n2g_ref[...]).astype(_BF) for a in accs]
    n_pass = D_FF // FFN_COLS
    for j in range(n_pass):
        lo = j * FFN_COLS
        w_gate = wfi_ref[:, lo:lo + FFN_COLS].astype(_BF)
        w_up = wfi_ref[:, D_FF + lo:D_FF + lo + FFN_COLS].astype(_BF)
        w_down = wfd_ref[lo:lo + FFN_COLS, :].astype(_BF)
        for k, r in enumerate(rows):
            act = (_silu(_dot(hns[k], w_gate)) * _dot(hns[k], w_up)).astype(_BF)
            accs[k] = accs[k] + _dot(act, w_down)
            if j == n_pass - 1:
                o_ref[r, :] = _rmsnorm(accs[k], nfg_ref[...])


def _resident(shape):
    nd = len(shape)
    return pl.BlockSpec(shape, lambda *_: (0,) * nd, pipeline_mode=pl.Buffered(1))


def _small(shape):
    nd = len(shape)
    return pl.BlockSpec(shape, lambda *_: (0,) * nd)


def _block_diag_groups(w):
    per = RG_BLOCKS // RG_GROUPS
    w = w.reshape(RG_GROUPS, per, RG_BLOCK, RG_BLOCK)
    eye = jnp.eye(per, dtype=w.dtype)
    bd = w[:, :, :, None, :] * eye[None, :, None, :, None]
    return bd.reshape(RG_GROUPS, RG_GROUP_W, RG_GROUP_W)


def kernel(x, meta_tokens, norm1_g, w_in, conv_w, conv_b, rg_wa, rg_ba, rg_wx, rg_bx,
           rg_lambda, hg_lb_logits, hg_norm_g, w_proj_a, w_proj_b, w_out, norm2_g,
           w_ffn_in, w_ffn_down, norm_f_g):
    nb, seq, d = x.shape
    assert d == D_MODEL and seq % TIME_TILE == 0 and (nb * seq) % FFN_ROWS == 0
    assert w_in.shape[0] == 1, "single-layer block"
    tt = TIME_TILE
    n_steps = 1 + seq // tt

    meta_tile = jnp.zeros((tt, d), x.dtype).at[tt - N_META:].set(meta_tokens.astype(x.dtype))
    w_in_b = w_in[0].astype(_BF)
    wg = jnp.concatenate([_block_diag_groups(rg_wa[0]), _block_diag_groups(rg_wx[0])],
                         axis=-1).astype(_BF)
    row = lambda v: v.reshape(1, -1).astype(_F32)

    x_map = lambda i: (0, jnp.maximum(i - 1, 0), 0)
    mixer = pl.pallas_call(
        functools.partial(_mixer_kernel, nb=nb, tt=tt),
        name="mixer",
        grid=(n_steps,),
        in_specs=[
            _resident((tt, d)),
            pl.BlockSpec((nb, tt, d), x_map),
            _small((1, d)),
            _resident(w_in_b.shape),
            _small((RG_CONV, RG_WIDTH)),
            _small((1, RG_WIDTH)),
            _resident(wg.shape),
            _small((1, RG_WIDTH)),
            _small((1, RG_WIDTH)),
            _small((1, RG_WIDTH)),
            _small((2, HG_WIDTH)),
            _small((1, HG_DK)),
            _resident((RG_WIDTH, d)),
            _resident((HG_WIDTH, d)),
            _resident((d, d)),
        ],
        out_specs=pl.BlockSpec((nb, tt, d), x_map),
        out_shape=jax.ShapeDtypeStruct((nb, seq, d), _F32),
        scratch_shapes=[
            pltpu.VMEM((nb, tt + SUBLANES, RG_WIDTH), _F32),
            pltpu.VMEM((nb, 1, RG_WIDTH), _F32),
            pltpu.VMEM((nb, HG_HEADS, HG_DK, HG_DK), _F32),
        ],
        compiler_params=pltpu.CompilerParams(
            dimension_semantics=("arbitrary",), vmem_limit_bytes=VMEM_LIMIT_BYTES),
    )
    h_mid = mixer(
        meta_tile, x, row(norm1_g[0]), w_in_b, conv_w[0].astype(_F32), row(conv_b[0]), wg,
        row(rg_ba[0]), row(rg_bx[0]), row(rg_lambda[0]), hg_lb_logits.astype(_F32),
        row(hg_norm_g[0]), w_proj_a[0].astype(_BF), w_proj_b[0].astype(_BF),
        w_out[0].astype(_BF))

    n_rows = nb * seq
    ffn = pl.pallas_call(
        _ffn_kernel,
        name="ffn",
        grid=(n_rows // FFN_ROWS,),
        in_specs=[
            pl.BlockSpec((FFN_ROWS, d), lambda i: (i, 0)),
            _small((1, d)),
            _resident((d, 2 * D_FF)),
            _resident((D_FF, d)),
            _small((1, d)),
        ],
        out_specs=pl.BlockSpec((FFN_ROWS, d), lambda i: (i, 0)),
        out_shape=jax.ShapeDtypeStruct((n_rows, d), _F32),
        compiler_params=pltpu.CompilerParams(
            dimension_semantics=("arbitrary",), vmem_limit_bytes=VMEM_LIMIT_BYTES),
    )
    out = ffn(h_mid.reshape(n_rows, d), row(norm2_g[0]), w_ffn_in[0].astype(_F32),
              w_ffn_down[0].astype(_F32), row(norm_f_g))
    return out.reshape(nb, seq, d)
```

```python
import functools
import math

import jax
import jax.numpy as jnp
from jax import lax
from jax.experimental import pallas as pl
from jax.experimental.pallas import tpu as pltpu

D_MODEL = 1024
N_META = 16
RG_WIDTH = 1280
RG_BLOCKS = 16
RG_BLOCK = 80
RG_GROUPS = 2
RG_GROUP_W = RG_WIDTH // RG_GROUPS
RG_CONV = 4
RG_C = 8.0
HG_HEADS = 8
HG_DK = 128
HG_WIDTH = 1024
D_FF = 2816
NORM_EPS = 1e-6
LOG2_E = math.log2(math.e)

OFF_AX, OFF_AG, OFF_Q, OFF_F, OFF_I, OFF_G, OFF_GA, OFF_GB = (
    0, 1280, 2560, 3584, 4608, 5632, 6656, 7680)

SUBLANES = 8
TIME_TILE = 128
CHUNK = 64
PROJ_COLS = 512
FFN_ROWS = 512
FFN_COLS = 1408
VMEM_LIMIT_BYTES = 60000 * 1024

_BF = jnp.bfloat16
_F32 = jnp.float32


def _dot(a, b):
    return jnp.dot(a, b, preferred_element_type=_F32)


def _dot_nt(a, b):
    return lax.dot_general(a, b, (((1,), (1,)), ((), ())), preferred_element_type=_F32)


def _dot_tn(a, b):
    return lax.dot_general(a, b, (((0,), (0,)), ((), ())), preferred_element_type=_F32)


def _rmsnorm(x, g):
    ms = jnp.mean(x * x, axis=-1, keepdims=True)
    return x * lax.rsqrt(ms + NORM_EPS) * g


def _sigmoid(x):
    return 0.5 * jnp.tanh(0.5 * x) + 0.5


def _silu(x):
    hx = 0.5 * x
    return hx * jnp.tanh(hx) + hx


def _sqrt_nonneg(x):
    return jnp.where(x > 0.0, x * lax.rsqrt(x), 0.0)


def _gelu_tanh(x):
    c = math.sqrt(2.0 / math.pi)
    return x * (0.5 * (1.0 + jnp.tanh(c * (x + 0.044715 * (x * x * x)))))


def _level_masks():
    t = lax.broadcasted_iota(jnp.int32, (CHUNK, CHUNK), 0)
    s = lax.broadcasted_iota(jnp.int32, (CHUNK, CHUNK), 1)
    masks = {0: t == s}
    h = 1
    while h < CHUNK:
        sh = int(math.log2(2 * h))
        masks[h] = ((t >> sh) == (s >> sh)) & ((t & h) != 0) & ((s & h) == 0)
        h *= 2
    groups = range(0, CHUNK, SUBLANES)
    return {h: [jnp.where(m[r:r + SUBLANES], 1.0, 0.0).astype(_F32) for r in groups]
            for h, m in masks.items()}


def _hgrn_operands(g2, q, k):
    w = g2.shape[-1]
    nv = CHUNK // SUBLANES
    sub = lax.broadcasted_iota(jnp.int32, (SUBLANES, w), 0)
    qs = [q[SUBLANES * j:SUBLANES * (j + 1)] for j in range(nv)]
    ks = [k[SUBLANES * j:SUBLANES * (j + 1)] for j in range(nv)]
    bcs = []
    run = None
    for j in range(nv):
        x = g2[SUBLANES * j:SUBLANES * (j + 1)]
        for s in (1, 2, 4):
            x = x + jnp.where(sub >= s, pltpu.roll(x, s, 0), 0.0)
        if run is not None:
            x = x + run
        bcs.append(x)
        run = x[SUBLANES - 1:SUBLANES]
    b_last = run

    def small_ref_row(h, j):
        x = bcs[j]
        rows = [jnp.broadcast_to(x[r:r + 1], x.shape)
                for r in range(h - 1, SUBLANES, 2 * h)]
        out = rows[-1]
        for idx in range(len(rows) - 2, -1, -1):
            out = jnp.where(sub < 2 * h * (idx + 1), rows[idx], out)
        return out

    levels = []
    h = CHUNK // 2
    while h >= 1:
        parts = []
        for j in range(nv):
            if h >= SUBLANES:
                r = (SUBLANES * j) // (2 * h) * (2 * h) + h - 1
                ref = bcs[r // SUBLANES][SUBLANES - 1:SUBLANES]
                if (SUBLANES * j) & h:
                    parts.append(qs[j] * jnp.exp2(bcs[j] - ref))
                else:
                    parts.append(ks[j] * jnp.exp2(ref - bcs[j]))
            else:
                base = jnp.where((sub & h) != 0, qs[j], ks[j])
                parts.append(base * jnp.exp2(-jnp.abs(bcs[j] - small_ref_row(h, j))))
        levels.append((h, jnp.concatenate(parts, axis=0).astype(_BF)))
        h //= 2
    q_bf = q.astype(_BF)
    k_bf = k.astype(_BF)
    q_dec = jnp.concatenate([qs[j] * jnp.exp2(bcs[j]) for j in range(nv)],
                            axis=0).astype(_BF)
    k_dec = jnp.concatenate([ks[j] * jnp.exp2(b_last - bcs[j]) for j in range(nv)],
                            axis=0).astype(_BF)
    s_dec = jnp.exp2(b_last)
    return levels, q_bf, k_bf, q_dec, k_dec, s_dec


def _hgrn_score_products(operands, masks, tick):
    levels, q_bf, k_bf, q_dec, k_dec, s_dec = operands
    nv = CHUNK // SUBLANES
    all_scores = []
    for hd in range(HG_HEADS):
        sl = slice(hd * HG_DK, (hd + 1) * HG_DK)
        diag = _dot_nt(q_bf[:, sl], k_bf[:, sl])
        acc = [diag[SUBLANES * j:SUBLANES * (j + 1)] * masks[0][j] for j in range(nv)]
        for h, x in levels:
            prod = _dot_nt(x[:, sl], x[:, sl])
            for j in range(nv):
                if h >= SUBLANES and not (SUBLANES * j) & h:
                    continue
                acc[j] = acc[j] + prod[SUBLANES * j:SUBLANES * (j + 1)] * masks[h][j]
        all_scores.append(jnp.concatenate(acc, axis=0).astype(_BF))
        if hd % 4 == 3:
            tick()
    return all_scores, q_dec, k_dec, s_dec


def _hgrn_apply(phase1, v_bf, st):
    all_scores, q_dec, k_dec, s_dec = phase1
    outs, new_st = [], []
    for hd in range(HG_HEADS):
        sl = slice(hd * HG_DK, (hd + 1) * HG_DK)
        o = _dot(all_scores[hd], v_bf[:, sl])
        o = o + _dot_nt(q_dec[:, sl], st[hd].astype(_BF))
        new_st.append(s_dec[:, sl] * st[hd] + _dot_tn(v_bf[:, sl], k_dec[:, sl]))
        outs.append(o)
    return jnp.concatenate(outs, axis=-1), new_st


def _projection_chunks():
    plan = (("ax", OFF_AX, RG_WIDTH), ("f", OFF_F, HG_WIDTH), ("q", OFF_Q, HG_WIDTH),
            ("v", OFF_I, HG_WIDTH), ("ag", OFF_AG, RG_WIDTH), ("ga", OFF_GA, D_MODEL),
            ("g", OFF_G, HG_WIDTH), ("gb", OFF_GB, D_MODEL))
    chunks = []
    for name, off, width in plan:
        for c in range(0, width, PROJ_COLS):
            chunks.append((name, off + c, min(PROJ_COLS, width - c)))
    return chunks


def _mixer_kernel(meta_ref, x_ref, n1g_ref, w_in_ref, convw_ref, convb_ref, wg_ref,
                  ba_ref, bx_ref, lam_ref, lbl_ref, hgn_ref, wpa_ref, wpb_ref, wout_ref,
                  o_ref, cbuf, hcar, st_ref, *, nb, tt):
    i = pl.program_id(0)
    rows = nb * tt
    is_meta = i == 0

    @pl.when(is_meta)
    def _():
        cbuf[...] = jnp.zeros_like(cbuf)
        hcar[...] = jnp.zeros_like(hcar)
        st_ref[...] = jnp.zeros_like(st_ref)

    meta = meta_ref[...]
    h_in = jnp.concatenate([jnp.where(is_meta, meta, x_ref[b]) for b in range(nb)], axis=0)
    hn = _rmsnorm(h_in, n1g_ref[...]).astype(_BF)

    chunks = _projection_chunks()
    issued = {}
    cursor = iter(chunks)

    def project_next(n=1):
        for _ in range(n):
            nxt = next(cursor, None)
            if nxt is None:
                return
            name, off, width = nxt
            issued.setdefault(name, []).append(_dot(hn, w_in_ref[:, off:off + width]))

    def projected(name):
        total = sum(1 for c in chunks if c[0] == name)
        while len(issued.get(name, ())) < total:
            project_next()
        return jnp.concatenate(issued[name], axis=-1)

    a_x = projected("ax")

    convw = convw_ref[...]
    xc_parts = []
    for b in range(nb):
        cbuf[b, SUBLANES:, :] = a_x[b * tt:(b + 1) * tt]
        acc = convb_ref[...] + convw[RG_CONV - 1:RG_CONV] * cbuf[b, pl.ds(SUBLANES, tt), :]
        for j in range(RG_CONV - 1):
            off = SUBLANES - (RG_CONV - 1) + j
            acc = acc + convw[j:j + 1] * cbuf[b, pl.ds(off, tt), :]
        xc_parts.append(acc)
        cbuf[b, 0:SUBLANES, :] = cbuf[b, pl.ds(tt, SUBLANES), :]
    xc = jnp.concatenate(xc_parts, axis=0)
    xc_bf = xc.astype(_BF)
    z = -lam_ref[...]
    softplus = jnp.maximum(z, 0.0) + jnp.log1p(jnp.exp(-jnp.abs(z)))
    a_scale = (-0.5 * RG_C * LOG2_E) * softplus
    a_parts, u_parts = [], []
    for gidx in range(RG_GROUPS):
        lo = gidx * RG_GROUP_W
        cols = slice(lo, lo + RG_GROUP_W)
        zz = _dot(xc_bf[:, cols], wg_ref[gidx])
        r_tanh = jnp.tanh(0.5 * (zz[:, :RG_GROUP_W] + ba_ref[:, cols]))
        a_g = jnp.exp2(a_scale[:, cols] * r_tanh + a_scale[:, cols])
        i_g = _sigmoid(zz[:, RG_GROUP_W:] + bx_ref[:, cols])
        a_parts.append(a_g)
        u_parts.append(_sqrt_nonneg(1.0 - a_g * a_g) * (i_g * xc[:, cols]))
        project_next()
    a = jnp.concatenate(a_parts, axis=-1)
    u = jnp.concatenate(u_parts, axis=-1)
    t_loc = lax.broadcasted_iota(jnp.int32, (rows, 1), 0) % tt
    u = jnp.where(jnp.logical_or(i > 0, t_loc >= tt - N_META), u, 0.0)

    sub = lax.broadcasted_iota(jnp.int32, (SUBLANES, RG_WIDTH), 0)
    h_parts = []
    for b in range(nb):
        carry = hcar[b]
        for j in range(tt // SUBLANES):
            r0 = b * tt + j * SUBLANES
            av = a[r0:r0 + SUBLANES]
            hv = u[r0:r0 + SUBLANES]
            for s in (1, 2, 4):
                keep = sub >= s
                hv = hv + av * jnp.where(keep, pltpu.roll(hv, s, 0), 0.0)
                av = av * jnp.where(keep, pltpu.roll(av, s, 0), 1.0)
            hv = hv + av * carry
            carry = hv[SUBLANES - 1:SUBLANES]
            h_parts.append(hv)
            if j % 8 == 7:
                project_next()
        hcar[b] = carry
    h_rg = jnp.concatenate(h_parts, axis=0)

    l0 = lbl_ref[0:1, :]
    l1 = lbl_ref[1:2, :]
    lmax = jnp.maximum(l0, l1)
    e0 = jnp.exp(l0 - lmax)
    e1 = jnp.exp(l1 - lmax)
    lb = e0 / (e0 + e1)
    f_sig = _sigmoid(projected("f"))
    log2_f = jnp.log(lb + (1.0 - lb) * f_sig) * LOG2_E
    k_in = (1.0 - lb) * (1.0 - f_sig)
    q_in = _silu(projected("q"))
    project_next()
    v_in = projected("v").astype(_BF)

    masks = _level_masks()
    order = [(b, c) for c in range(tt // CHUNK) for b in range(nb)]
    st = {b: [st_ref[b, hd] for hd in range(HG_HEADS)] for b in range(nb)}
    o_chunks = {}

    def rows_of(b, c):
        r0 = b * tt + c * CHUNK
        return slice(r0, r0 + CHUNK)

    def apply_phase(b, c, phase1):
        o_chunks[(b, c)], st[b] = _hgrn_apply(phase1, v_in[rows_of(b, c)], st[b])

    prepared = []
    for b, c in order:
        sl = rows_of(b, c)
        prepared.append(_hgrn_operands(log2_f[sl], q_in[sl], k_in[sl]))
        project_next()
    pending = None
    for (b, c), operands in zip(order, prepared):
        phase1 = _hgrn_score_products(operands, masks, project_next)
        if pending is not None:
            apply_phase(*pending)
        pending = (b, c, phase1)
    apply_phase(*pending)
    for b in range(nb):
        for hd in range(HG_HEADS):
            st_ref[b, hd] = st[b][hd]
    o_b = jnp.concatenate([o_chunks[(b, c)] for b in range(nb) for c in range(tt // CHUNK)],
                          axis=0)

    y_a = (h_rg * _gelu_tanh(projected("ag"))).astype(_BF)
    merged = _sigmoid(projected("ga")) * _dot(y_a, wpa_ref[...].astype(_BF))
    g_act = _silu(projected("g"))
    hgn = hgn_ref[...]
    yb_parts = []
    for hd in range(HG_HEADS):
        sl = slice(hd * HG_DK, (hd + 1) * HG_DK)
        yb_parts.append((_rmsnorm(o_b[:, sl], hgn) * g_act[:, sl]).astype(_BF))
    y_b = jnp.concatenate(yb_parts, axis=-1)
    merged = merged + _sigmoid(projected("gb")) * _dot(y_b, wpb_ref[...].astype(_BF))

    out = h_in + _dot(merged.astype(_BF), wout_ref[...].astype(_BF))
    for b in range(nb):
        o_ref[b] = out[b * tt:(b + 1) * tt]


def _ffn_kernel(h_ref, n2g_ref, wfi_ref, wfd_ref, nfg_ref, o_ref):
    h = h_ref[...]
    hn = _rmsnorm(h, n2g_ref[...]).astype(_BF)
    acc = h
    for j in range(D_FF // FFN_COLS):
        lo = j * FFN_COLS
        gcol = _dot(hn, wfi_ref[:, lo:lo + FFN_COLS].astype(_BF))
        ucol = _dot(hn, wfi_ref[:, D_FF + lo:D_FF + lo + FFN_COLS].astype(_BF))
        act = (_silu(gcol) * ucol).astype(_BF)
        acc = acc + _dot(act, wfd_ref[lo:lo + FFN_COLS, :].astype(_BF))
    o_ref[...] = _rmsnorm(acc, nfg_ref[...])


def _resident(shape):
    nd = len(shape)
    return pl.BlockSpec(shape, lambda *_: (0,) * nd, pipeline_mode=pl.Buffered(1))


def _small(shape):
    nd = len(shape)
    return pl.BlockSpec(shape, lambda *_: (0,) * nd)


def _block_diag_groups(w):
    per = RG_BLOCKS // RG_GROUPS
    w = w.reshape(RG_GROUPS, per, RG_BLOCK, RG_BLOCK)
    eye = jnp.eye(per, dtype=w.dtype)
    bd = w[:, :, :, None, :] * eye[None, :, None, :, None]
    return bd.reshape(RG_GROUPS, RG_GROUP_W, RG_GROUP_W)


def kernel(x, meta_tokens, norm1_g, w_in, conv_w, conv_b, rg_wa, rg_ba, rg_wx, rg_bx,
           rg_lambda, hg_lb_logits, hg_norm_g, w_proj_a, w_proj_b, w_out, norm2_g,
           w_ffn_in, w_ffn_down, norm_f_g):
    nb, seq, d = x.shape
    assert d == D_MODEL and seq % TIME_TILE == 0 and (nb * seq) % FFN_ROWS == 0
    assert w_in.shape[0] == 1, "single-layer block"
    tt = TIME_TILE
    n_steps = 1 + seq // tt

    meta_tile = jnp.zeros((tt, d), x.dtype).at[tt - N_META:].set(meta_tokens.astype(x.dtype))
    w_in_b = w_in[0].astype(_BF)
    wg = jnp.concatenate([_block_diag_groups(rg_wa[0]), _block_diag_groups(rg_wx[0])],
                         axis=-1).astype(_BF)
    row = lambda v: v.reshape(1, -1).astype(_F32)

    x_map = lambda i: (0, jnp.maximum(i - 1, 0), 0)
    mixer = pl.pallas_call(
        functools.partial(_mixer_kernel, nb=nb, tt=tt),
        name="mixer",
        grid=(n_steps,),
        in_specs=[
            _resident((tt, d)),
            pl.BlockSpec((nb, tt, d), x_map),
            _small((1, d)),
            _resident(w_in_b.shape),
            _small((RG_CONV, RG_WIDTH)),
            _small((1, RG_WIDTH)),
            _resident(wg.shape),
            _small((1, RG_WIDTH)),
            _small((1, RG_WIDTH)),
            _small((1, RG_WIDTH)),
            _small((2, HG_WIDTH)),
            _small((1, HG_DK)),
            _resident((RG_WIDTH, d)),
            _resident((HG_WIDTH, d)),
            _resident((d, d)),
        ],
        out_specs=pl.BlockSpec((nb, tt, d), x_map),
        out_shape=jax.ShapeDtypeStruct((nb, seq, d), _F32),
        scratch_shapes=[
            pltpu.VMEM((nb, tt + SUBLANES, RG_WIDTH), _F32),
            pltpu.VMEM((nb, 1, RG_WIDTH), _F32),
            pltpu.VMEM((nb, HG_HEADS, HG_DK, HG_DK), _F32),
        ],
        compiler_params=pltpu.CompilerParams(
            dimension_semantics=("arbitrary",), vmem_limit_bytes=VMEM_LIMIT_BYTES),
    )
    h_mid = mixer(
        meta_tile, x, row(norm1_g[0]), w_in_b, conv_w[0].astype(_F32), row(conv_b[0]), wg,
        row(rg_ba[0]), row(rg_bx[0]), row(rg_lambda[0]), hg_lb_logits.astype(_F32),
        row(hg_norm_g[0]), w_proj_a[0].astype(_F32), w_proj_b[0].astype(_F32),
        w_out[0].astype(_F32))

    n_rows = nb * seq
    ffn = pl.pallas_call(
        _ffn_kernel,
        name="ffn",
        grid=(n_rows // FFN_ROWS,),
        in_specs=[
            pl.BlockSpec((FFN_ROWS, d), lambda i: (i, 0)),
            _small((1, d)),
            _resident((d, 2 * D_FF)),
            _resident((D_FF, d)),
            _small((1, d)),
        ],
        out_specs=pl.BlockSpec((FFN_ROWS, d), lambda i: (i, 0)),
        out_shape=jax.ShapeDtypeStruct((n_rows, d), _F32),
        compiler_params=pltpu.CompilerParams(
            dimension_semantics=("arbitrary",), vmem_limit_bytes=VMEM_LIMIT_BYTES),
    )
    out = ffn(h_mid.reshape(n_rows, d), row(norm2_g[0]), w_ffn_in[0].astype(_F32),
              w_ffn_down[0].astype(_F32), row(norm_f_g))
    return out.reshape(nb, seq, d)
```

```python
import functools
import math

import jax
import jax.numpy as jnp
from jax import lax
from jax.experimental import pallas as pl
from jax.experimental.pallas import tpu as pltpu

D_MODEL = 1024
N_META = 16
RG_WIDTH = 1280
RG_BLOCKS = 16
RG_BLOCK = 80
RG_GROUPS = 2
RG_GROUP_W = RG_WIDTH // RG_GROUPS
RG_CONV = 4
RG_C = 8.0
HG_HEADS = 8
HG_DK = 128
HG_WIDTH = 1024
D_FF = 2816
NORM_EPS = 1e-6
LOG2_E = math.log2(math.e)

OFF_AX, OFF_AG, OFF_Q, OFF_F, OFF_I, OFF_G, OFF_GA, OFF_GB = (
    0, 1280, 2560, 3584, 4608, 5632, 6656, 7680)

SUBLANES = 8
TIME_TILE = 128
CHUNK = 64
PROJ_COLS = 512
FFN_ROWS = 512
FFN_COLS = 1408
VMEM_LIMIT_BYTES = 60000 * 1024

_BF = jnp.bfloat16
_F32 = jnp.float32


def _dot(a, b):
    return jnp.dot(a, b, preferred_element_type=_F32)


def _dot_nt(a, b):
    return lax.dot_general(a, b, (((1,), (1,)), ((), ())), preferred_element_type=_F32)


def _dot_tn(a, b):
    return lax.dot_general(a, b, (((0,), (0,)), ((), ())), preferred_element_type=_F32)


def _rmsnorm(x, g):
    ms = jnp.mean(x * x, axis=-1, keepdims=True)
    return x * lax.rsqrt(ms + NORM_EPS) * g


def _sigmoid(x):
    return 0.5 * jnp.tanh(0.5 * x) + 0.5


def _silu(x):
    hx = 0.5 * x
    return hx * jnp.tanh(hx) + hx


def _sqrt_nonneg(x):
    return jnp.where(x > 0.0, x * lax.rsqrt(x), 0.0)


def _gelu_tanh(x):
    c = math.sqrt(2.0 / math.pi)
    return x * (0.5 * (1.0 + jnp.tanh(c * (x + 0.044715 * (x * x * x)))))


def _level_masks():
    t = lax.broadcasted_iota(jnp.int32, (CHUNK, CHUNK), 0)
    s = lax.broadcasted_iota(jnp.int32, (CHUNK, CHUNK), 1)
    masks = {0: t == s}
    h = 1
    while h < CHUNK:
        sh = int(math.log2(2 * h))
        masks[h] = ((t >> sh) == (s >> sh)) & ((t & h) != 0) & ((s & h) == 0)
        h *= 2
    groups = range(0, CHUNK, SUBLANES)
    return {h: [jnp.where(m[r:r + SUBLANES], 1.0, 0.0).astype(_F32) for r in groups]
            for h, m in masks.items()}


def _hgrn_operands(g2, q, k):
    w = g2.shape[-1]
    nv = CHUNK // SUBLANES
    sub = lax.broadcasted_iota(jnp.int32, (SUBLANES, w), 0)
    qs = [q[SUBLANES * j:SUBLANES * (j + 1)] for j in range(nv)]
    ks = [k[SUBLANES * j:SUBLANES * (j + 1)] for j in range(nv)]
    bcs = []
    run = None
    for j in range(nv):
        x = g2[SUBLANES * j:SUBLANES * (j + 1)]
        for s in (1, 2, 4):
            x = x + jnp.where(sub >= s, pltpu.roll(x, s, 0), 0.0)
        if run is not None:
            x = x + run
        bcs.append(x)
        run = x[SUBLANES - 1:SUBLANES]
    b_last = run

    def small_ref_row(h, j):
        x = bcs[j]
        rows = [jnp.broadcast_to(x[r:r + 1], x.shape)
                for r in range(h - 1, SUBLANES, 2 * h)]
        out = rows[-1]
        for idx in range(len(rows) - 2, -1, -1):
            out = jnp.where(sub < 2 * h * (idx + 1), rows[idx], out)
        return out

    levels = []
    h = CHUNK // 2
    while h >= 1:
        parts = []
        for j in range(nv):
            if h >= SUBLANES:
                r = (SUBLANES * j) // (2 * h) * (2 * h) + h - 1
                ref = bcs[r // SUBLANES][SUBLANES - 1:SUBLANES]
                if (SUBLANES * j) & h:
                    parts.append(qs[j] * jnp.exp2(bcs[j] - ref))
                else:
                    parts.append(ks[j] * jnp.exp2(ref - bcs[j]))
            else:
                base = jnp.where((sub & h) != 0, qs[j], ks[j])
                parts.append(base * jnp.exp2(-jnp.abs(bcs[j] - small_ref_row(h, j))))
        levels.append((h, jnp.concatenate(parts, axis=0).astype(_BF)))
        h //= 2
    q_bf = q.astype(_BF)
    k_bf = k.astype(_BF)
    q_dec = jnp.concatenate([qs[j] * jnp.exp2(bcs[j]) for j in range(nv)],
                            axis=0).astype(_BF)
    k_dec = jnp.concatenate([ks[j] * jnp.exp2(b_last - bcs[j]) for j in range(nv)],
                            axis=0).astype(_BF)
    s_dec = jnp.exp2(b_last)
    return levels, q_bf, k_bf, q_dec, k_dec, s_dec


def _hgrn_score_products(operands, masks, tick):
    levels, q_bf, k_bf, q_dec, k_dec, s_dec = operands
    nv = CHUNK // SUBLANES
    all_scores = []
    for hd in range(HG_HEADS):
        sl = slice(hd * HG_DK, (hd + 1) * HG_DK)
        diag = _dot_nt(q_bf[:, sl], k_bf[:, sl])
        acc = [diag[SUBLANES * j:SUBLANES * (j + 1)] * masks[0][j] for j in range(nv)]
        for h, x in levels:
            prod = _dot_nt(x[:, sl], x[:, sl])
            for j in range(nv):
                if h >= SUBLANES and not (SUBLANES * j) & h:
                    continue
                acc[j] = acc[j] + prod[SUBLANES * j:SUBLANES * (j + 1)] * masks[h][j]
        all_scores.append(jnp.concatenate(acc, axis=0).astype(_BF))
        if hd % 4 == 3:
            tick()
    return all_scores, q_dec, k_dec, s_dec


def _hgrn_apply(phase1, v_bf, st):
    all_scores, q_dec, k_dec, s_dec = phase1
    outs, new_st = [], []
    for hd in range(HG_HEADS):
        sl = slice(hd * HG_DK, (hd + 1) * HG_DK)
        o = _dot(all_scores[hd], v_bf[:, sl])
        o = o + _dot_nt(q_dec[:, sl], st[hd].astype(_BF))
        new_st.append(s_dec[:, sl] * st[hd] + _dot_tn(v_bf[:, sl], k_dec[:, sl]))
        outs.append(o)
    return jnp.concatenate(outs, axis=-1), new_st


def _projection_chunks():
    plan = (("ax", OFF_AX, RG_WIDTH), ("f", OFF_F, HG_WIDTH), ("q", OFF_Q, HG_WIDTH),
            ("v", OFF_I, HG_WIDTH), ("ag", OFF_AG, RG_WIDTH), ("ga", OFF_GA, D_MODEL),
            ("g", OFF_G, HG_WIDTH), ("gb", OFF_GB, D_MODEL))
    chunks = []
    for name, off, width in plan:
        for c in range(0, width, PROJ_COLS):
            chunks.append((name, off + c, min(PROJ_COLS, width - c)))
    return chunks


def _mixer_kernel(meta_ref, x_ref, n1g_ref, w_in_ref, convw_ref, convb_ref, wa_ref, wx_ref,
                  ba_ref, bx_ref, lam_ref, lbl_ref, hgn_ref, wpa_ref, wpb_ref, wout_ref,
                  o_ref, wg_buf, cbuf, hcar, st_ref, *, nb, tt):
    i = pl.program_id(0)
    rows = nb * tt
    is_meta = i == 0

    @pl.when(is_meta)
    def _():
        cbuf[...] = jnp.zeros_like(cbuf)
        hcar[...] = jnp.zeros_like(hcar)
        st_ref[...] = jnp.zeros_like(st_ref)
        kk = lax.broadcasted_iota(jnp.int32, (RG_BLOCK, RG_GROUP_W), 0)
        cc = lax.broadcasted_iota(jnp.int32, (RG_BLOCK, RG_GROUP_W), 1)
        tile_cols = functools.reduce(
            jnp.logical_or, [cc == kk + RG_BLOCK * n for n in range(RG_GROUP_W // RG_BLOCK)])
        tiling = jnp.where(tile_cols, 1.0, 0.0).astype(_BF)
        ri = lax.broadcasted_iota(jnp.int32, (RG_GROUP_W, RG_GROUP_W), 0)
        ci = lax.broadcasted_iota(jnp.int32, (RG_GROUP_W, RG_GROUP_W), 1)
        same_block = functools.reduce(
            jnp.logical_or,
            [jnp.logical_and(jnp.logical_and(ri >= RG_BLOCK * n, ri < RG_BLOCK * (n + 1)),
                             jnp.logical_and(ci >= RG_BLOCK * n, ci < RG_BLOCK * (n + 1)))
             for n in range(RG_GROUP_W // RG_BLOCK)])
        for gidx in range(RG_GROUPS):
            for col, w_ref in ((0, wa_ref), (RG_GROUP_W, wx_ref)):
                spread = _dot(w_ref[gidx].astype(_BF), tiling)
                wg_buf[gidx, :, col:col + RG_GROUP_W] = jnp.where(
                    same_block, spread, 0.0).astype(_BF)

    meta = meta_ref[...]
    h_in = jnp.concatenate([jnp.where(is_meta, meta, x_ref[b]) for b in range(nb)], axis=0)
    hn = _rmsnorm(h_in, n1g_ref[...]).astype(_BF)

    chunks = _projection_chunks()
    issued = {}
    cursor = iter(chunks)

    def project_next(n=1):
        for _ in range(n):
            nxt = next(cursor, None)
            if nxt is None:
                return
            name, off, width = nxt
            issued.setdefault(name, []).append(_dot(hn, w_in_ref[:, off:off + width]))

    def projected(name):
        total = sum(1 for c in chunks if c[0] == name)
        while len(issued.get(name, ())) < total:
            project_next()
        return jnp.concatenate(issued[name], axis=-1)

    a_x = projected("ax")

    convw = convw_ref[...]
    xc_parts = []
    for b in range(nb):
        cbuf[b, SUBLANES:, :] = a_x[b * tt:(b + 1) * tt]
        acc = convb_ref[...] + convw[RG_CONV - 1:RG_CONV] * cbuf[b, pl.ds(SUBLANES, tt), :]
        for j in range(RG_CONV - 1):
            off = SUBLANES - (RG_CONV - 1) + j
            acc = acc + convw[j:j + 1] * cbuf[b, pl.ds(off, tt), :]
        xc_parts.append(acc)
        cbuf[b, 0:SUBLANES, :] = cbuf[b, pl.ds(tt, SUBLANES), :]
    xc = jnp.concatenate(xc_parts, axis=0)
    xc_bf = xc.astype(_BF)
    z = -lam_ref[...]
    softplus = jnp.maximum(z, 0.0) + jnp.log1p(jnp.exp(-jnp.abs(z)))
    a_scale = (-0.5 * RG_C * LOG2_E) * softplus
    a_parts, u_parts = [], []
    for gidx in range(RG_GROUPS):
        lo = gidx * RG_GROUP_W
        cols = slice(lo, lo + RG_GROUP_W)
        zz = _dot(xc_bf[:, cols], wg_buf[gidx])
        r_tanh = jnp.tanh(0.5 * (zz[:, :RG_GROUP_W] + ba_ref[:, cols]))
        a_g = jnp.exp2(a_scale[:, cols] * r_tanh + a_scale[:, cols])
        i_g = _sigmoid(zz[:, RG_GROUP_W:] + bx_ref[:, cols])
        a_parts.append(a_g)
        u_parts.append(_sqrt_nonneg(1.0 - a_g * a_g) * (i_g * xc[:, cols]))
        project_next()
    a = jnp.concatenate(a_parts, axis=-1)
    u = jnp.concatenate(u_parts, axis=-1)
    t_loc = lax.broadcasted_iota(jnp.int32, (rows, 1), 0) % tt
    u = jnp.where(jnp.logical_or(i > 0, t_loc >= tt - N_META), u, 0.0)

    sub = lax.broadcasted_iota(jnp.int32, (SUBLANES, RG_WIDTH), 0)
    h_parts = []
    for b in range(nb):
        carry = hcar[b]
        for j in range(tt // SUBLANES):
            r0 = b * tt + j * SUBLANES
            av = a[r0:r0 + SUBLANES]
            hv = u[r0:r0 + SUBLANES]
            for s in (1, 2, 4):
                keep = sub >= s
                hv = hv + av * jnp.where(keep, pltpu.roll(hv, s, 0), 0.0)
                av = av * jnp.where(keep, pltpu.roll(av, s, 0), 1.0)
            hv = hv + av * carry
            carry = hv[SUBLANES - 1:SUBLANES]
            h_parts.append(hv)
            if j % 8 == 7:
                project_next()
        hcar[b] = carry
    h_rg = jnp.concatenate(h_parts, axis=0)

    l0 = lbl_ref[0:1, :]
    l1 = lbl_ref[1:2, :]
    lmax = jnp.maximum(l0, l1)
    e0 = jnp.exp(l0 - lmax)
    e1 = jnp.exp(l1 - lmax)
    lb = e0 / (e0 + e1)
    f_sig = _sigmoid(projected("f"))
    log2_f = jnp.log(lb + (1.0 - lb) * f_sig) * LOG2_E
    k_in = (1.0 - lb) * (1.0 - f_sig)
    q_in = _silu(projected("q"))
    project_next()
    v_in = projected("v").astype(_BF)

    masks = _level_masks()
    order = [(b, c) for c in range(tt // CHUNK) for b in range(nb)]
    st = {b: [st_ref[b, hd] for hd in range(HG_HEADS)] for b in range(nb)}
    o_chunks = {}

    def rows_of(b, c):
        r0 = b * tt + c * CHUNK
        return slice(r0, r0 + CHUNK)

    def apply_phase(b, c, phase1):
        o_chunks[(b, c)], st[b] = _hgrn_apply(phase1, v_in[rows_of(b, c)], st[b])

    prepared = []
    for b, c in order:
        sl = rows_of(b, c)
        prepared.append(_hgrn_operands(log2_f[sl], q_in[sl], k_in[sl]))
        project_next()
    pending = None
    for (b, c), operands in zip(order, prepared):
        phase1 = _hgrn_score_products(operands, masks, project_next)
        if pending is not None:
            apply_phase(*pending)
        pending = (b, c, phase1)
    apply_phase(*pending)
    for b in range(nb):
        for hd in range(HG_HEADS):
            st_ref[b, hd] = st[b][hd]
    o_b = jnp.concatenate([o_chunks[(b, c)] for b in range(nb) for c in range(tt // CHUNK)],
                          axis=0)

    y_a = (h_rg * _gelu_tanh(projected("ag"))).astype(_BF)
    merged = _sigmoid(projected("ga")) * _dot(y_a, wpa_ref[...].astype(_BF))
    g_act = _silu(projected("g"))
    hgn = hgn_ref[...]
    yb_parts = []
    for hd in range(HG_HEADS):
        sl = slice(hd * HG_DK, (hd + 1) * HG_DK)
        yb_parts.append((_rmsnorm(o_b[:, sl], hgn) * g_act[:, sl]).astype(_BF))
    y_b = jnp.concatenate(yb_parts, axis=-1)
    merged = merged + _sigmoid(projected("gb")) * _dot(y_b, wpb_ref[...].astype(_BF))

    out = h_in + _dot(merged.astype(_BF), wout_ref[...].astype(_BF))
    for b in range(nb):
        o_ref[b] = out[b * tt:(b + 1) * tt]


def _ffn_kernel(h_ref, n2g_ref, wfi_ref, wfd_ref, nfg_ref, o_ref):
    h = h_ref[...]
    hn = _rmsnorm(h, n2g_ref[...]).astype(_BF)
    acc = h
    for j in range(D_FF // FFN_COLS):
        lo = j * FFN_COLS
        gcol = _dot(hn, wfi_ref[:, lo:lo + FFN_COLS].astype(_BF))
        ucol = _dot(hn, wfi_ref[:, D_FF + lo:D_FF + lo + FFN_COLS].astype(_BF))
        act = (_silu(gcol) * ucol).astype(_BF)
        acc = acc + _dot(act, wfd_ref[lo:lo + FFN_COLS, :].astype(_BF))
    o_ref[...] = _rmsnorm(acc, nfg_ref[...])


def _resident(shape):
    nd = len(shape)
    return pl.BlockSpec(shape, lambda *_: (0,) * nd, pipeline_mode=pl.Buffered(1))


def _small(shape):
    nd = len(shape)
    return pl.BlockSpec(shape, lambda *_: (0,) * nd)


def kernel(x, meta_tokens, norm1_g, w_in, conv_w, conv_b, rg_wa, rg_ba, rg_wx, rg_bx,
           rg_lambda, hg_lb_logits, hg_norm_g, w_proj_a, w_proj_b, w_out, norm2_g,
           w_ffn_in, w_ffn_down, norm_f_g):
    nb, seq, d = x.shape
    assert d == D_MODEL and seq % TIME_TILE == 0 and (nb * seq) % FFN_ROWS == 0
    assert w_in.shape[0] == 1, "single-layer block"
    tt = TIME_TILE
    n_steps = 1 + seq // tt

    meta_tile = jnp.zeros((tt, d), x.dtype).at[tt - N_META:].set(meta_tokens.astype(x.dtype))
    w_in_b = w_in[0].astype(_BF)
    stack_blocks = lambda w: w.reshape(RG_GROUPS, RG_GROUP_W, RG_BLOCK).astype(_F32)
    row = lambda v: v.reshape(1, -1).astype(_F32)

    x_map = lambda i: (0, jnp.maximum(i - 1, 0), 0)
    mixer = pl.pallas_call(
        functools.partial(_mixer_kernel, nb=nb, tt=tt),
        name="mixer",
        grid=(n_steps,),
        in_specs=[
            _resident((tt, d)),
            pl.BlockSpec((nb, tt, d), x_map),
            _small((1, d)),
            _resident(w_in_b.shape),
            _small((RG_CONV, RG_WIDTH)),
            _small((1, RG_WIDTH)),
            _small((RG_GROUPS, RG_GROUP_W, RG_BLOCK)),
            _small((RG_GROUPS, RG_GROUP_W, RG_BLOCK)),
            _small((1, RG_WIDTH)),
            _small((1, RG_WIDTH)),
            _small((1, RG_WIDTH)),
            _small((2, HG_WIDTH)),
            _small((1, HG_DK)),
            _resident((RG_WIDTH, d)),
            _resident((HG_WIDTH, d)),
            _resident((d, d)),
        ],
        out_specs=pl.BlockSpec((nb, tt, d), x_map),
        out_shape=jax.ShapeDtypeStruct((nb, seq, d), _F32),
        scratch_shapes=[
            pltpu.VMEM((RG_GROUPS, RG_GROUP_W, 2 * RG_GROUP_W), _BF),
            pltpu.VMEM((nb, tt + SUBLANES, RG_WIDTH), _F32),
            pltpu.VMEM((nb, 1, RG_WIDTH), _F32),
            pltpu.VMEM((nb, HG_HEADS, HG_DK, HG_DK), _F32),
        ],
        compiler_params=pltpu.CompilerParams(
            dimension_semantics=("arbitrary",), vmem_limit_bytes=VMEM_LIMIT_BYTES),
    )
    h_mid = mixer(
        meta_tile, x, row(norm1_g[0]), w_in_b, conv_w[0].astype(_F32), row(conv_b[0]),
        stack_blocks(rg_wa[0]), stack_blocks(rg_wx[0]),
        row(rg_ba[0]), row(rg_bx[0]), row(rg_lambda[0]), hg_lb_logits.astype(_F32),
        row(hg_norm_g[0]), w_proj_a[0].astype(_F32), w_proj_b[0].astype(_F32),
        w_out[0].astype(_F32))

    n_rows = nb * seq
    ffn = pl.pallas_call(
        _ffn_kernel,
        name="ffn",
        grid=(n_rows // FFN_ROWS,),
        in_specs=[
            pl.BlockSpec((FFN_ROWS, d), lambda i: (i, 0)),
            _small((1, d)),
            _resident((d, 2 * D_FF)),
            _resident((D_FF, d)),
            _small((1, d)),
        ],
        out_specs=pl.BlockSpec((FFN_ROWS, d), lambda i: (i, 0)),
        out_shape=jax.ShapeDtypeStruct((n_rows, d), _F32),
        compiler_params=pltpu.CompilerParams(
            dimension_semantics=("arbitrary",), vmem_limit_bytes=VMEM_LIMIT_BYTES),
    )
    out = ffn(h_mid.reshape(n_rows, d), row(norm2_g[0]), w_ffn_in[0].astype(_F32),
              w_ffn_down[0].astype(_F32), row(norm_f_g))
    return out.reshape(nb, seq, d)
```

```python
import functools
import math

import jax
import jax.numpy as jnp
from jax import lax
from jax.experimental import pallas as pl
from jax.experimental.pallas import tpu as pltpu

D_MODEL = 1024
N_META = 16
RG_WIDTH = 1280
RG_BLOCKS = 16
RG_BLOCK = 80
RG_GROUPS = 2
RG_GROUP_W = RG_WIDTH // RG_GROUPS
RG_CONV = 4
RG_C = 8.0
HG_HEADS = 8
HG_DK = 128
HG_WIDTH = 1024
D_FF = 2816
NORM_EPS = 1e-6
LOG2_E = math.log2(math.e)

OFF_AX, OFF_AG, OFF_Q, OFF_F, OFF_I, OFF_G, OFF_GA, OFF_GB = (
    0, 1280, 2560, 3584, 4608, 5632, 6656, 7680)

SUBLANES = 8
TIME_TILE = 128
CHUNK = 64
PROJ_COLS = 512
FFN_ROWS = 512
FFN_COLS = 1408
VMEM_LIMIT_BYTES = 60000 * 1024

_BF = jnp.bfloat16
_F32 = jnp.float32


def _dot(a, b):
    return jnp.dot(a, b, preferred_element_type=_F32)


def _dot_nt(a, b):
    return lax.dot_general(a, b, (((1,), (1,)), ((), ())), preferred_element_type=_F32)


def _dot_tn(a, b):
    return lax.dot_general(a, b, (((0,), (0,)), ((), ())), preferred_element_type=_F32)


def _rmsnorm(x, g):
    ms = jnp.mean(x * x, axis=-1, keepdims=True)
    return x * lax.rsqrt(ms + NORM_EPS) * g


def _sigmoid(x):
    return 0.5 * jnp.tanh(0.5 * x) + 0.5


def _silu(x):
    hx = 0.5 * x
    return hx * jnp.tanh(hx) + hx


def _sqrt_nonneg(x):
    return jnp.where(x > 0.0, x * lax.rsqrt(x), 0.0)


def _gelu_tanh(x):
    c = math.sqrt(2.0 / math.pi)
    return x * (0.5 * (1.0 + jnp.tanh(c * (x + 0.044715 * (x * x * x)))))


def _level_masks():
    t = lax.broadcasted_iota(jnp.int32, (CHUNK, CHUNK), 0)
    s = lax.broadcasted_iota(jnp.int32, (CHUNK, CHUNK), 1)
    masks = {0: t == s}
    h = 1
    while h < CHUNK:
        sh = int(math.log2(2 * h))
        masks[h] = ((t >> sh) == (s >> sh)) & ((t & h) != 0) & ((s & h) == 0)
        h *= 2
    groups = range(0, CHUNK, SUBLANES)
    return {h: [jnp.where(m[r:r + SUBLANES], 1.0, 0.0).astype(_F32) for r in groups]
            for h, m in masks.items()}


def _hgrn_operands(g2, q, k):
    w = g2.shape[-1]
    nv = CHUNK // SUBLANES
    sub = lax.broadcasted_iota(jnp.int32, (SUBLANES, w), 0)
    qs = [q[SUBLANES * j:SUBLANES * (j + 1)] for j in range(nv)]
    ks = [k[SUBLANES * j:SUBLANES * (j + 1)] for j in range(nv)]
    bcs = []
    run = None
    for j in range(nv):
        x = g2[SUBLANES * j:SUBLANES * (j + 1)]
        for s in (1, 2, 4):
            x = x + jnp.where(sub >= s, pltpu.roll(x, s, 0), 0.0)
        if run is not None:
            x = x + run
        bcs.append(x)
        run = x[SUBLANES - 1:SUBLANES]
    b_last = run

    def small_ref_row(h, j):
        x = bcs[j]
        rows = [jnp.broadcast_to(x[r:r + 1], x.shape)
                for r in range(h - 1, SUBLANES, 2 * h)]
        out = rows[-1]
        for idx in range(len(rows) - 2, -1, -1):
            out = jnp.where(sub < 2 * h * (idx + 1), rows[idx], out)
        return out

    levels = []
    h = CHUNK // 2
    while h >= 1:
        parts = []
        for j in range(nv):
            if h >= SUBLANES:
                r = (SUBLANES * j) // (2 * h) * (2 * h) + h - 1
                ref = bcs[r // SUBLANES][SUBLANES - 1:SUBLANES]
                if (SUBLANES * j) & h:
                    parts.append(qs[j] * jnp.exp2(bcs[j] - ref))
                else:
                    parts.append(ks[j] * jnp.exp2(ref - bcs[j]))
            else:
                base = jnp.where((sub & h) != 0, qs[j], ks[j])
                parts.append(base * jnp.exp2(-jnp.abs(bcs[j] - small_ref_row(h, j))))
        levels.append((h, jnp.concatenate(parts, axis=0).astype(_BF)))
        h //= 2
    q_bf = q.astype(_BF)
    k_bf = k.astype(_BF)
    q_dec = jnp.concatenate([qs[j] * jnp.exp2(bcs[j]) for j in range(nv)],
                            axis=0).astype(_BF)
    k_dec = jnp.concatenate([ks[j] * jnp.exp2(b_last - bcs[j]) for j in range(nv)],
                            axis=0).astype(_BF)
    s_dec = jnp.exp2(b_last)
    return levels, q_bf, k_bf, q_dec, k_dec, s_dec


def _hgrn_score_products(operands, masks, tick):
    levels, q_bf, k_bf, q_dec, k_dec, s_dec = operands
    nv = CHUNK // SUBLANES
    all_scores = []
    for hd in range(HG_HEADS):
        sl = slice(hd * HG_DK, (hd + 1) * HG_DK)
        diag = _dot_nt(q_bf[:, sl], k_bf[:, sl])
        acc = [diag[SUBLANES * j:SUBLANES * (j + 1)] * masks[0][j] for j in range(nv)]
        for h, x in levels:
            prod = _dot_nt(x[:, sl], x[:, sl])
            for j in range(nv):
                if h >= SUBLANES and not (SUBLANES * j) & h:
                    continue
                acc[j] = acc[j] + prod[SUBLANES * j:SUBLANES * (j + 1)] * masks[h][j]
        all_scores.append(jnp.concatenate(acc, axis=0).astype(_BF))
        if hd % 4 == 3:
            tick()
    return all_scores, q_dec, k_dec, s_dec


def _hgrn_apply(phase1, v_bf, st):
    all_scores, q_dec, k_dec, s_dec = phase1
    outs, new_st = [], []
    for hd in range(HG_HEADS):
        sl = slice(hd * HG_DK, (hd + 1) * HG_DK)
        o = _dot(all_scores[hd], v_bf[:, sl])
        o = o + _dot_nt(q_dec[:, sl], st[hd].astype(_BF))
        new_st.append(s_dec[:, sl] * st[hd] + _dot_tn(v_bf[:, sl], k_dec[:, sl]))
        outs.append(o)
    return jnp.concatenate(outs, axis=-1), new_st


def _projection_chunks():
    plan = (("ax", OFF_AX, RG_WIDTH), ("f", OFF_F, HG_WIDTH), ("q", OFF_Q, HG_WIDTH),
            ("v", OFF_I, HG_WIDTH), ("ag", OFF_AG, RG_WIDTH), ("ga", OFF_GA, D_MODEL),
            ("g", OFF_G, HG_WIDTH), ("gb", OFF_GB, D_MODEL))
    chunks = []
    for name, off, width in plan:
        for c in range(0, width, PROJ_COLS):
            chunks.append((name, off + c, min(PROJ_COLS, width - c)))
    return chunks


def _mixer_kernel(meta_ref, x_ref, n1g_ref, w_in_hbm, convw_ref, convb_ref, wa_ref, wx_ref,
                  ba_ref, bx_ref, lam_ref, lbl_ref, hgn_ref, wpa_ref, wpb_ref, wout_ref,
                  o_ref, w_in_ref, w_stage, w_sems, wg_buf, cbuf, hcar, st_ref, *, nb, tt):
    i = pl.program_id(0)
    rows = nb * tt
    is_meta = i == 0

    @pl.when(is_meta)
    def _():
        n_chunks = w_in_ref.shape[1] // PROJ_COLS

        def w_copy(c):
            return pltpu.make_async_copy(w_in_hbm.at[:, pl.ds(c * PROJ_COLS, PROJ_COLS)],
                                         w_stage.at[c % 2], w_sems.at[c % 2])

        w_copy(0).start()
        for c in range(n_chunks):
            if c + 1 < n_chunks:
                w_copy(c + 1).start()
            w_copy(c).wait()
            w_in_ref[:, c * PROJ_COLS:(c + 1) * PROJ_COLS] = w_stage[c % 2].astype(_BF)
        cbuf[...] = jnp.zeros_like(cbuf)
        hcar[...] = jnp.zeros_like(hcar)
        st_ref[...] = jnp.zeros_like(st_ref)
        kk = lax.broadcasted_iota(jnp.int32, (RG_BLOCK, RG_GROUP_W), 0)
        cc = lax.broadcasted_iota(jnp.int32, (RG_BLOCK, RG_GROUP_W), 1)
        tile_cols = functools.reduce(
            jnp.logical_or, [cc == kk + RG_BLOCK * n for n in range(RG_GROUP_W // RG_BLOCK)])
        tiling = jnp.where(tile_cols, 1.0, 0.0).astype(_BF)
        ri = lax.broadcasted_iota(jnp.int32, (RG_GROUP_W, RG_GROUP_W), 0)
        ci = lax.broadcasted_iota(jnp.int32, (RG_GROUP_W, RG_GROUP_W), 1)
        same_block = functools.reduce(
            jnp.logical_or,
            [jnp.logical_and(jnp.logical_and(ri >= RG_BLOCK * n, ri < RG_BLOCK * (n + 1)),
                             jnp.logical_and(ci >= RG_BLOCK * n, ci < RG_BLOCK * (n + 1)))
             for n in range(RG_GROUP_W // RG_BLOCK)])
        for gidx in range(RG_GROUPS):
            for col, w_ref in ((0, wa_ref), (RG_GROUP_W, wx_ref)):
                spread = _dot(w_ref[gidx].astype(_BF), tiling)
                wg_buf[gidx, :, col:col + RG_GROUP_W] = jnp.where(
                    same_block, spread, 0.0).astype(_BF)

    meta = meta_ref[...]
    h_in = jnp.concatenate([jnp.where(is_meta, meta, x_ref[b]) for b in range(nb)], axis=0)
    hn = _rmsnorm(h_in, n1g_ref[...]).astype(_BF)

    chunks = _projection_chunks()
    issued = {}
    cursor = iter(chunks)

    def project_next(n=1):
        for _ in range(n):
            nxt = next(cursor, None)
            if nxt is None:
                return
            name, off, width = nxt
            issued.setdefault(name, []).append(_dot(hn, w_in_ref[:, off:off + width]))

    def projected(name):
        total = sum(1 for c in chunks if c[0] == name)
        while len(issued.get(name, ())) < total:
            project_next()
        return jnp.concatenate(issued[name], axis=-1)

    a_x = projected("ax")

    convw = convw_ref[...]
    xc_parts = []
    for b in range(nb):
        cbuf[b, SUBLANES:, :] = a_x[b * tt:(b + 1) * tt]
        acc = convb_ref[...] + convw[RG_CONV - 1:RG_CONV] * cbuf[b, pl.ds(SUBLANES, tt), :]
        for j in range(RG_CONV - 1):
            off = SUBLANES - (RG_CONV - 1) + j
            acc = acc + convw[j:j + 1] * cbuf[b, pl.ds(off, tt), :]
        xc_parts.append(acc)
        cbuf[b, 0:SUBLANES, :] = cbuf[b, pl.ds(tt, SUBLANES), :]
    xc = jnp.concatenate(xc_parts, axis=0)
    xc_bf = xc.astype(_BF)
    z = -lam_ref[...]
    softplus = jnp.maximum(z, 0.0) + jnp.log1p(jnp.exp(-jnp.abs(z)))
    a_scale = (-0.5 * RG_C * LOG2_E) * softplus
    a_parts, u_parts = [], []
    for gidx in range(RG_GROUPS):
        lo = gidx * RG_GROUP_W
        cols = slice(lo, lo + RG_GROUP_W)
        zz = _dot(xc_bf[:, cols], wg_buf[gidx])
        r_tanh = jnp.tanh(0.5 * (zz[:, :RG_GROUP_W] + ba_ref[:, cols]))
        a_g = jnp.exp2(a_scale[:, cols] * r_tanh + a_scale[:, cols])
        i_g = _sigmoid(zz[:, RG_GROUP_W:] + bx_ref[:, cols])
        a_parts.append(a_g)
        u_parts.append(_sqrt_nonneg(1.0 - a_g * a_g) * (i_g * xc[:, cols]))
        project_next()
    a = jnp.concatenate(a_parts, axis=-1)
    u = jnp.concatenate(u_parts, axis=-1)
    t_loc = lax.broadcasted_iota(jnp.int32, (rows, 1), 0) % tt
    u = jnp.where(jnp.logical_or(i > 0, t_loc >= tt - N_META), u, 0.0)

    sub = lax.broadcasted_iota(jnp.int32, (SUBLANES, RG_WIDTH), 0)
    h_parts = []
    for b in range(nb):
        carry = hcar[b]
        for j in range(tt // SUBLANES):
            r0 = b * tt + j * SUBLANES
            av = a[r0:r0 + SUBLANES]
            hv = u[r0:r0 + SUBLANES]
            for s in (1, 2, 4):
                keep = sub >= s
                hv = hv + av * jnp.where(keep, pltpu.roll(hv, s, 0), 0.0)
                av = av * jnp.where(keep, pltpu.roll(av, s, 0), 1.0)
            hv = hv + av * carry
            carry = hv[SUBLANES - 1:SUBLANES]
            h_parts.append(hv)
            if j % 8 == 7:
                project_next()
        hcar[b] = carry
    h_rg = jnp.concatenate(h_parts, axis=0)

    l0 = lbl_ref[0:1, :]
    l1 = lbl_ref[1:2, :]
    lmax = jnp.maximum(l0, l1)
    e0 = jnp.exp(l0 - lmax)
    e1 = jnp.exp(l1 - lmax)
    lb = e0 / (e0 + e1)
    f_sig = _sigmoid(projected("f"))
    log2_f = jnp.log(lb + (1.0 - lb) * f_sig) * LOG2_E
    k_in = (1.0 - lb) * (1.0 - f_sig)
    q_in = _silu(projected("q"))
    project_next()
    v_in = projected("v").astype(_BF)

    masks = _level_masks()
    order = [(b, c) for c in range(tt // CHUNK) for b in range(nb)]
    st = {b: [st_ref[b, hd] for hd in range(HG_HEADS)] for b in range(nb)}
    o_chunks = {}

    def rows_of(b, c):
        r0 = b * tt + c * CHUNK
        return slice(r0, r0 + CHUNK)

    def apply_phase(b, c, phase1):
        o_chunks[(b, c)], st[b] = _hgrn_apply(phase1, v_in[rows_of(b, c)], st[b])

    prepared = []
    for b, c in order:
        sl = rows_of(b, c)
        prepared.append(_hgrn_operands(log2_f[sl], q_in[sl], k_in[sl]))
        project_next()
    pending = None
    for (b, c), operands in zip(order, prepared):
        phase1 = _hgrn_score_products(operands, masks, project_next)
        if pending is not None:
            apply_phase(*pending)
        pending = (b, c, phase1)
    apply_phase(*pending)
    for b in range(nb):
        for hd in range(HG_HEADS):
            st_ref[b, hd] = st[b][hd]
    o_b = jnp.concatenate([o_chunks[(b, c)] for b in range(nb) for c in range(tt // CHUNK)],
                          axis=0)

    y_a = (h_rg * _gelu_tanh(projected("ag"))).astype(_BF)
    merged = _sigmoid(projected("ga")) * _dot(y_a, wpa_ref[...].astype(_BF))
    g_act = _silu(projected("g"))
    hgn = hgn_ref[...]
    yb_parts = []
    for hd in range(HG_HEADS):
        sl = slice(hd * HG_DK, (hd + 1) * HG_DK)
        yb_parts.append((_rmsnorm(o_b[:, sl], hgn) * g_act[:, sl]).astype(_BF))
    y_b = jnp.concatenate(yb_parts, axis=-1)
    merged = merged + _sigmoid(projected("gb")) * _dot(y_b, wpb_ref[...].astype(_BF))

    out = h_in + _dot(merged.astype(_BF), wout_ref[...].astype(_BF))
    for b in range(nb):
        o_ref[b] = out[b * tt:(b + 1) * tt]


def _ffn_kernel(h_ref, n2g_ref, wfi_ref, wfd_ref, nfg_ref, o_ref):
    h = h_ref[...]
    hn = _rmsnorm(h, n2g_ref[...]).astype(_BF)
    acc = h
    for j in range(D_FF // FFN_COLS):
        lo = j * FFN_COLS
        gcol = _dot(hn, wfi_ref[:, lo:lo + FFN_COLS].astype(_BF))
        ucol = _dot(hn, wfi_ref[:, D_FF + lo:D_FF + lo + FFN_COLS].astype(_BF))
        act = (_silu(gcol) * ucol).astype(_BF)
        acc = acc + _dot(act, wfd_ref[lo:lo + FFN_COLS, :].astype(_BF))
    o_ref[...] = _rmsnorm(acc, nfg_ref[...])


def _resident(shape):
    nd = len(shape)
    return pl.BlockSpec(shape, lambda *_: (0,) * nd, pipeline_mode=pl.Buffered(1))


def _small(shape):
    nd = len(shape)
    return pl.BlockSpec(shape, lambda *_: (0,) * nd)


def kernel(x, meta_tokens, norm1_g, w_in, conv_w, conv_b, rg_wa, rg_ba, rg_wx, rg_bx,
           rg_lambda, hg_lb_logits, hg_norm_g, w_proj_a, w_proj_b, w_out, norm2_g,
           w_ffn_in, w_ffn_down, norm_f_g):
    nb, seq, d = x.shape
    assert d == D_MODEL and seq % TIME_TILE == 0 and (nb * seq) % FFN_ROWS == 0
    assert w_in.shape[0] == 1, "single-layer block"
    tt = TIME_TILE
    n_steps = 1 + seq // tt

    meta_tile = jnp.zeros((tt, d), x.dtype).at[tt - N_META:].set(meta_tokens.astype(x.dtype))
    w_in_f = w_in[0].astype(_F32)
    assert w_in_f.shape[1] % PROJ_COLS == 0
    stack_blocks = lambda w: w.reshape(RG_GROUPS, RG_GROUP_W, RG_BLOCK).astype(_F32)
    row = lambda v: v.reshape(1, -1).astype(_F32)

    x_map = lambda i: (0, jnp.maximum(i - 1, 0), 0)
    mixer = pl.pallas_call(
        functools.partial(_mixer_kernel, nb=nb, tt=tt),
        name="mixer",
        grid=(n_steps,),
        in_specs=[
            _resident((tt, d)),
            pl.BlockSpec((nb, tt, d), x_map),
            _small((1, d)),
            pl.BlockSpec(memory_space=pl.ANY),
            _small((RG_CONV, RG_WIDTH)),
            _small((1, RG_WIDTH)),
            _small((RG_GROUPS, RG_GROUP_W, RG_BLOCK)),
            _small((RG_GROUPS, RG_GROUP_W, RG_BLOCK)),
            _small((1, RG_WIDTH)),
            _small((1, RG_WIDTH)),
            _small((1, RG_WIDTH)),
            _small((2, HG_WIDTH)),
            _small((1, HG_DK)),
            _resident((RG_WIDTH, d)),
            _resident((HG_WIDTH, d)),
            _resident((d, d)),
        ],
        out_specs=pl.BlockSpec((nb, tt, d), x_map),
        out_shape=jax.ShapeDtypeStruct((nb, seq, d), _F32),
        scratch_shapes=[
            pltpu.VMEM(w_in_f.shape, _BF),
            pltpu.VMEM((2, d, PROJ_COLS), _F32),
            pltpu.SemaphoreType.DMA((2,)),
            pltpu.VMEM((RG_GROUPS, RG_GROUP_W, 2 * RG_GROUP_W), _BF),
            pltpu.VMEM((nb, tt + SUBLANES, RG_WIDTH), _F32),
            pltpu.VMEM((nb, 1, RG_WIDTH), _F32),
            pltpu.VMEM((nb, HG_HEADS, HG_DK, HG_DK), _F32),
        ],
        compiler_params=pltpu.CompilerParams(
            dimension_semantics=("arbitrary",), vmem_limit_bytes=VMEM_LIMIT_BYTES),
    )
    h_mid = mixer(
        meta_tile, x, row(norm1_g[0]), w_in_f, conv_w[0].astype(_F32), row(conv_b[0]),
        stack_blocks(rg_wa[0]), stack_blocks(rg_wx[0]),
        row(rg_ba[0]), row(rg_bx[0]), row(rg_lambda[0]), hg_lb_logits.astype(_F32),
        row(hg_norm_g[0]), w_proj_a[0].astype(_F32), w_proj_b[0].astype(_F32),
        w_out[0].astype(_F32))

    n_rows = nb * seq
    ffn = pl.pallas_call(
        _ffn_kernel,
        name="ffn",
        grid=(n_rows // FFN_ROWS,),
        in_specs=[
            pl.BlockSpec((FFN_ROWS, d), lambda i: (i, 0)),
            _small((1, d)),
            _resident((d, 2 * D_FF)),
            _resident((D_FF, d)),
            _small((1, d)),
        ],
        out_specs=pl.BlockSpec((FFN_ROWS, d), lambda i: (i, 0)),
        out_shape=jax.ShapeDtypeStruct((n_rows, d), _F32),
        compiler_params=pltpu.CompilerParams(
            dimension_semantics=("arbitrary",), vmem_limit_bytes=VMEM_LIMIT_BYTES),
    )
    out = ffn(h_mid.reshape(n_rows, d), row(norm2_g[0]), w_ffn_in[0].astype(_F32),
              w_ffn_down[0].astype(_F32), row(norm_f_g))
    return out.reshape(nb, seq, d)
```

```python
import functools
import math

import jax
import jax.numpy as jnp
from jax import lax
from jax.experimental import pallas as pl
from jax.experimental.pallas import tpu as pltpu

D_MODEL = 1024
N_META = 16
RG_WIDTH = 1280
RG_BLOCKS = 16
RG_BLOCK = 80
RG_GROUPS = 2
RG_GROUP_W = RG_WIDTH // RG_GROUPS
RG_CONV = 4
RG_C = 8.0
HG_HEADS = 8
HG_DK = 128
HG_WIDTH = 1024
D_FF = 2816
NORM_EPS = 1e-6
LOG2_E = math.log2(math.e)

OFF_AX, OFF_AG, OFF_Q, OFF_F, OFF_I, OFF_G, OFF_GA, OFF_GB = (
    0, 1280, 2560, 3584, 4608, 5632, 6656, 7680)

SUBLANES = 8
TIME_TILE = 128
CHUNK = 64
FAST_BLOCK = 8
FAST_LOG2_RANGE = 96.0
PROJ_COLS = 512
FFN_ROWS = 512
FFN_COLS = 1408
VMEM_LIMIT_BYTES = 60000 * 1024

_BF = jnp.bfloat16
_F32 = jnp.float32


def _dot(a, b):
    return jnp.dot(a, b, preferred_element_type=_F32)


def _dot_nt(a, b):
    return lax.dot_general(a, b, (((1,), (1,)), ((), ())), preferred_element_type=_F32)


def _dot_tn(a, b):
    return lax.dot_general(a, b, (((0,), (0,)), ((), ())), preferred_element_type=_F32)


def _rmsnorm(x, g):
    ms = jnp.mean(x * x, axis=-1, keepdims=True)
    return x * lax.rsqrt(ms + NORM_EPS) * g


def _sigmoid(x):
    return 0.5 * jnp.tanh(0.5 * x) + 0.5


def _silu(x):
    hx = 0.5 * x
    return hx * jnp.tanh(hx) + hx


def _sqrt_nonneg(x):
    return jnp.where(x > 0.0, x * lax.rsqrt(x), 0.0)


def _gelu_tanh(x):
    c = math.sqrt(2.0 / math.pi)
    return x * (0.5 * (1.0 + jnp.tanh(c * (x + 0.044715 * (x * x * x)))))


def _level_masks():
    t = lax.broadcasted_iota(jnp.int32, (CHUNK, CHUNK), 0)
    s = lax.broadcasted_iota(jnp.int32, (CHUNK, CHUNK), 1)
    masks = {0: t == s}
    h = 1
    while h < CHUNK:
        sh = int(math.log2(2 * h))
        masks[h] = ((t >> sh) == (s >> sh)) & ((t & h) != 0) & ((s & h) == 0)
        h *= 2
    masks["block"] = ((t // FAST_BLOCK) == (s // FAST_BLOCK)) & (s <= t)
    groups = range(0, CHUNK, SUBLANES)
    return {h: [jnp.where(m[r:r + SUBLANES], 1.0, 0.0).astype(_F32) for r in groups]
            for h, m in masks.items()}


def _hgrn_operands(g2, q, k, fast):
    w = g2.shape[-1]
    nv = CHUNK // SUBLANES
    sub = lax.broadcasted_iota(jnp.int32, (SUBLANES, w), 0)
    qs = [q[SUBLANES * j:SUBLANES * (j + 1)] for j in range(nv)]
    ks = [k[SUBLANES * j:SUBLANES * (j + 1)] for j in range(nv)]
    bcs = []
    run = None
    for j in range(nv):
        x = g2[SUBLANES * j:SUBLANES * (j + 1)]
        for s in (1, 2, 4):
            x = x + jnp.where(sub >= s, pltpu.roll(x, s, 0), 0.0)
        if run is not None:
            x = x + run
        bcs.append(x)
        run = x[SUBLANES - 1:SUBLANES]
    b_last = run

    def small_ref_row(h, j):
        x = bcs[j]
        rows = [jnp.broadcast_to(x[r:r + 1], x.shape)
                for r in range(h - 1, SUBLANES, 2 * h)]
        out = rows[-1]
        for idx in range(len(rows) - 2, -1, -1):
            out = jnp.where(sub < 2 * h * (idx + 1), rows[idx], out)
        return out

    levels = []
    h = CHUNK // 2
    while h >= (FAST_BLOCK if fast else 1):
        parts = []
        for j in range(nv):
            if h >= SUBLANES:
                r = (SUBLANES * j) // (2 * h) * (2 * h) + h - 1
                ref = bcs[r // SUBLANES][SUBLANES - 1:SUBLANES]
                if (SUBLANES * j) & h:
                    parts.append(qs[j] * jnp.exp2(bcs[j] - ref))
                else:
                    parts.append(ks[j] * jnp.exp2(ref - bcs[j]))
            else:
                base = jnp.where((sub & h) != 0, qs[j], ks[j])
                parts.append(base * jnp.exp2(-jnp.abs(bcs[j] - small_ref_row(h, j))))
        levels.append((h, jnp.concatenate(parts, axis=0).astype(_BF)))
        h //= 2
    if fast:
        pq, pk = [], []
        for j in range(nv):
            e = bcs[j] if j == 0 else bcs[j] - bcs[j - 1][SUBLANES - 1:SUBLANES]
            pq.append(qs[j] * jnp.exp2(e))
            pk.append(ks[j] * jnp.exp2(-e))
        q_bf = jnp.concatenate(pq, axis=0).astype(_BF)
        k_bf = jnp.concatenate(pk, axis=0).astype(_BF)
    else:
        q_bf = q.astype(_BF)
        k_bf = k.astype(_BF)
    q_dec = jnp.concatenate([qs[j] * jnp.exp2(bcs[j]) for j in range(nv)],
                            axis=0).astype(_BF)
    k_dec = jnp.concatenate([ks[j] * jnp.exp2(b_last - bcs[j]) for j in range(nv)],
                            axis=0).astype(_BF)
    s_dec = jnp.exp2(b_last)
    return levels, q_bf, k_bf, q_dec, k_dec, s_dec


def _hgrn_score_products(operands, masks, tick, fast):
    levels, q_bf, k_bf, q_dec, k_dec, s_dec = operands
    nv = CHUNK // SUBLANES
    all_scores = []
    for hd in range(HG_HEADS):
        sl = slice(hd * HG_DK, (hd + 1) * HG_DK)
        diag = _dot_nt(q_bf[:, sl], k_bf[:, sl])
        near = masks["block"] if fast else masks[0]
        acc = [diag[SUBLANES * j:SUBLANES * (j + 1)] * near[j] for j in range(nv)]
        for h, x in levels:
            prod = _dot_nt(x[:, sl], x[:, sl])
            for j in range(nv):
                if h >= SUBLANES and not (SUBLANES * j) & h:
                    continue
                acc[j] = acc[j] + prod[SUBLANES * j:SUBLANES * (j + 1)] * masks[h][j]
        all_scores.append(jnp.concatenate(acc, axis=0).astype(_BF))
        if hd % 4 == 3:
            tick()
    return all_scores, q_dec, k_dec, s_dec


def _hgrn_apply(phase1, v_bf, st):
    all_scores, q_dec, k_dec, s_dec = phase1
    outs, new_st = [], []
    for hd in range(HG_HEADS):
        sl = slice(hd * HG_DK, (hd + 1) * HG_DK)
        o = _dot(all_scores[hd], v_bf[:, sl])
        o = o + _dot_nt(q_dec[:, sl], st[hd].astype(_BF))
        new_st.append(s_dec[:, sl] * st[hd] + _dot_tn(v_bf[:, sl], k_dec[:, sl]))
        outs.append(o)
    return jnp.concatenate(outs, axis=-1), new_st


def _projection_chunks():
    plan = (("ax", OFF_AX, RG_WIDTH), ("f", OFF_F, HG_WIDTH), ("q", OFF_Q, HG_WIDTH),
            ("v", OFF_I, HG_WIDTH), ("ag", OFF_AG, RG_WIDTH), ("ga", OFF_GA, D_MODEL),
            ("g", OFF_G, HG_WIDTH), ("gb", OFF_GB, D_MODEL))
    chunks = []
    for name, off, width in plan:
        for c in range(0, width, PROJ_COLS):
            chunks.append((name, off + c, min(PROJ_COLS, width - c)))
    return chunks


def _mixer_kernel(meta_ref, x_ref, n1g_ref, w_in_hbm, convw_ref, convb_ref, wa_ref, wx_ref,
                  ba_ref, bx_ref, lam_ref, lbl_ref, hgn_ref, wpa_ref, wpb_ref, wout_ref,
                  o_ref, w_in_ref, w_stage, w_sems, wg_buf, cbuf, hcar, st_ref, *, nb, tt,
                  fast):
    i = pl.program_id(0)
    rows = nb * tt
    is_meta = i == 0

    @pl.when(is_meta)
    def _():
        n_chunks = w_in_ref.shape[1] // PROJ_COLS

        def w_copy(c):
            return pltpu.make_async_copy(w_in_hbm.at[:, pl.ds(c * PROJ_COLS, PROJ_COLS)],
                                         w_stage.at[c % 2], w_sems.at[c % 2])

        w_copy(0).start()
        for c in range(n_chunks):
            if c + 1 < n_chunks:
                w_copy(c + 1).start()
            w_copy(c).wait()
            w_in_ref[:, c * PROJ_COLS:(c + 1) * PROJ_COLS] = w_stage[c % 2].astype(_BF)
        cbuf[...] = jnp.zeros_like(cbuf)
        hcar[...] = jnp.zeros_like(hcar)
        st_ref[...] = jnp.zeros_like(st_ref)
        kk = lax.broadcasted_iota(jnp.int32, (RG_BLOCK, RG_GROUP_W), 0)
        cc = lax.broadcasted_iota(jnp.int32, (RG_BLOCK, RG_GROUP_W), 1)
        tile_cols = functools.reduce(
            jnp.logical_or, [cc == kk + RG_BLOCK * n for n in range(RG_GROUP_W // RG_BLOCK)])
        tiling = jnp.where(tile_cols, 1.0, 0.0).astype(_BF)
        ri = lax.broadcasted_iota(jnp.int32, (RG_GROUP_W, RG_GROUP_W), 0)
        ci = lax.broadcasted_iota(jnp.int32, (RG_GROUP_W, RG_GROUP_W), 1)
        same_block = functools.reduce(
            jnp.logical_or,
            [jnp.logical_and(jnp.logical_and(ri >= RG_BLOCK * n, ri < RG_BLOCK * (n + 1)),
                             jnp.logical_and(ci >= RG_BLOCK * n, ci < RG_BLOCK * (n + 1)))
             for n in range(RG_GROUP_W // RG_BLOCK)])
        for gidx in range(RG_GROUPS):
            for col, w_ref in ((0, wa_ref), (RG_GROUP_W, wx_ref)):
                spread = _dot(w_ref[gidx].astype(_BF), tiling)
                wg_buf[gidx, :, col:col + RG_GROUP_W] = jnp.where(
                    same_block, spread, 0.0).astype(_BF)

    meta = meta_ref[...]
    h_in = jnp.concatenate([jnp.where(is_meta, meta, x_ref[b]) for b in range(nb)], axis=0)
    hn = _rmsnorm(h_in, n1g_ref[...]).astype(_BF)

    chunks = _projection_chunks()
    issued = {}
    cursor = iter(chunks)

    def project_next(n=1):
        for _ in range(n):
            nxt = next(cursor, None)
            if nxt is None:
                return
            name, off, width = nxt
            issued.setdefault(name, []).append(_dot(hn, w_in_ref[:, off:off + width]))

    def projected(name):
        total = sum(1 for c in chunks if c[0] == name)
        while len(issued.get(name, ())) < total:
            project_next()
        return jnp.concatenate(issued[name], axis=-1)

    a_x = projected("ax")

    convw = convw_ref[...]
    xc_parts = []
    for b in range(nb):
        cbuf[b, SUBLANES:, :] = a_x[b * tt:(b + 1) * tt]
        acc = convb_ref[...] + convw[RG_CONV - 1:RG_CONV] * cbuf[b, pl.ds(SUBLANES, tt), :]
        for j in range(RG_CONV - 1):
            off = SUBLANES - (RG_CONV - 1) + j
            acc = acc + convw[j:j + 1] * cbuf[b, pl.ds(off, tt), :]
        xc_parts.append(acc)
        cbuf[b, 0:SUBLANES, :] = cbuf[b, pl.ds(tt, SUBLANES), :]
    xc = jnp.concatenate(xc_parts, axis=0)
    xc_bf = xc.astype(_BF)
    z = -lam_ref[...]
    softplus = jnp.maximum(z, 0.0) + jnp.log1p(jnp.exp(-jnp.abs(z)))
    a_scale = (-0.5 * RG_C * LOG2_E) * softplus
    a_parts, u_parts = [], []
    for gidx in range(RG_GROUPS):
        lo = gidx * RG_GROUP_W
        cols = slice(lo, lo + RG_GROUP_W)
        zz = _dot(xc_bf[:, cols], wg_buf[gidx])
        r_tanh = jnp.tanh(0.5 * (zz[:, :RG_GROUP_W] + ba_ref[:, cols]))
        a_g = jnp.exp2(a_scale[:, cols] * r_tanh + a_scale[:, cols])
        i_g = _sigmoid(zz[:, RG_GROUP_W:] + bx_ref[:, cols])
        a_parts.append(a_g)
        u_parts.append(_sqrt_nonneg(1.0 - a_g * a_g) * (i_g * xc[:, cols]))
        project_next()
    a = jnp.concatenate(a_parts, axis=-1)
    u = jnp.concatenate(u_parts, axis=-1)
    t_loc = lax.broadcasted_iota(jnp.int32, (rows, 1), 0) % tt
    u = jnp.where(jnp.logical_or(i > 0, t_loc >= tt - N_META), u, 0.0)

    sub = lax.broadcasted_iota(jnp.int32, (SUBLANES, RG_WIDTH), 0)
    h_parts = []
    for b in range(nb):
        carry = hcar[b]
        for j in range(tt // SUBLANES):
            r0 = b * tt + j * SUBLANES
            av = a[r0:r0 + SUBLANES]
            hv = u[r0:r0 + SUBLANES]
            for s in (1, 2, 4):
                keep = sub >= s
                hv = hv + av * jnp.where(keep, pltpu.roll(hv, s, 0), 0.0)
                av = av * jnp.where(keep, pltpu.roll(av, s, 0), 1.0)
            hv = hv + av * carry
            carry = hv[SUBLANES - 1:SUBLANES]
            h_parts.append(hv)
            if j % 8 == 7:
                project_next()
        hcar[b] = carry
    h_rg = jnp.concatenate(h_parts, axis=0)

    l0 = lbl_ref[0:1, :]
    l1 = lbl_ref[1:2, :]
    lmax = jnp.maximum(l0, l1)
    e0 = jnp.exp(l0 - lmax)
    e1 = jnp.exp(l1 - lmax)
    lb = e0 / (e0 + e1)
    f_sig = _sigmoid(projected("f"))
    log2_f = jnp.log(lb + (1.0 - lb) * f_sig) * LOG2_E
    k_in = (1.0 - lb) * (1.0 - f_sig)
    q_in = _silu(projected("q"))
    project_next()
    v_in = projected("v").astype(_BF)

    masks = _level_masks()
    order = [(b, c) for c in range(tt // CHUNK) for b in range(nb)]
    st = {b: [st_ref[b, hd] for hd in range(HG_HEADS)] for b in range(nb)}
    o_chunks = {}

    def rows_of(b, c):
        r0 = b * tt + c * CHUNK
        return slice(r0, r0 + CHUNK)

    def apply_phase(b, c, phase1):
        o_chunks[(b, c)], st[b] = _hgrn_apply(phase1, v_in[rows_of(b, c)], st[b])

    prepared = []
    for b, c in order:
        sl = rows_of(b, c)
        prepared.append(_hgrn_operands(log2_f[sl], q_in[sl], k_in[sl], fast))
        project_next()
    pending = None
    for (b, c), operands in zip(order, prepared):
        phase1 = _hgrn_score_products(operands, masks, project_next, fast)
        if pending is not None:
            apply_phase(*pending)
        pending = (b, c, phase1)
    apply_phase(*pending)
    for b in range(nb):
        for hd in range(HG_HEADS):
            st_ref[b, hd] = st[b][hd]
    o_b = jnp.concatenate([o_chunks[(b, c)] for b in range(nb) for c in range(tt // CHUNK)],
                          axis=0)

    y_a = (h_rg * _gelu_tanh(projected("ag"))).astype(_BF)
    merged = _sigmoid(projected("ga")) * _dot(y_a, wpa_ref[...].astype(_BF))
    g_act = _silu(projected("g"))
    hgn = hgn_ref[...]
    yb_parts = []
    for hd in range(HG_HEADS):
        sl = slice(hd * HG_DK, (hd + 1) * HG_DK)
        yb_parts.append((_rmsnorm(o_b[:, sl], hgn) * g_act[:, sl]).astype(_BF))
    y_b = jnp.concatenate(yb_parts, axis=-1)
    merged = merged + _sigmoid(projected("gb")) * _dot(y_b, wpb_ref[...].astype(_BF))

    out = h_in + _dot(merged.astype(_BF), wout_ref[...].astype(_BF))
    for b in range(nb):
        o_ref[b] = out[b * tt:(b + 1) * tt]


def _ffn_kernel(h_ref, n2g_ref, wfi_ref, wfd_ref, nfg_ref, o_ref):
    h = h_ref[...]
    hn = _rmsnorm(h, n2g_ref[...]).astype(_BF)
    acc = h
    for j in range(D_FF // FFN_COLS):
        lo = j * FFN_COLS
        gcol = _dot(hn, wfi_ref[:, lo:lo + FFN_COLS].astype(_BF))
        ucol = _dot(hn, wfi_ref[:, D_FF + lo:D_FF + lo + FFN_COLS].astype(_BF))
        act = (_silu(gcol) * ucol).astype(_BF)
        acc = acc + _dot(act, wfd_ref[lo:lo + FFN_COLS, :].astype(_BF))
    o_ref[...] = _rmsnorm(acc, nfg_ref[...])


def _resident(shape):
    nd = len(shape)
    return pl.BlockSpec(shape, lambda *_: (0,) * nd, pipeline_mode=pl.Buffered(1))


def _small(shape):
    nd = len(shape)
    return pl.BlockSpec(shape, lambda *_: (0,) * nd)


def kernel(x, meta_tokens, norm1_g, w_in, conv_w, conv_b, rg_wa, rg_ba, rg_wx, rg_bx,
           rg_lambda, hg_lb_logits, hg_norm_g, w_proj_a, w_proj_b, w_out, norm2_g,
           w_ffn_in, w_ffn_down, norm_f_g):
    nb, seq, d = x.shape
    assert d == D_MODEL and seq % TIME_TILE == 0 and (nb * seq) % FFN_ROWS == 0
    assert w_in.shape[0] == 1, "single-layer block"
    tt = TIME_TILE
    n_steps = 1 + seq // tt

    meta_tile = jnp.zeros((tt, d), x.dtype).at[tt - N_META:].set(meta_tokens.astype(x.dtype))
    w_in_f = w_in[0].astype(_F32)
    assert w_in_f.shape[1] % PROJ_COLS == 0
    stack_blocks = lambda w: w.reshape(RG_GROUPS, RG_GROUP_W, RG_BLOCK).astype(_F32)
    row = lambda v: v.reshape(1, -1).astype(_F32)

    x_map = lambda i: (0, jnp.maximum(i - 1, 0), 0)
    make_mixer = lambda fast: pl.pallas_call(
        functools.partial(_mixer_kernel, nb=nb, tt=tt, fast=fast),
        name="mixer_fast" if fast else "mixer",
        grid=(n_steps,),
        in_specs=[
            _resident((tt, d)),
            pl.BlockSpec((nb, tt, d), x_map),
            _small((1, d)),
            pl.BlockSpec(memory_space=pl.ANY),
            _small((RG_CONV, RG_WIDTH)),
            _small((1, RG_WIDTH)),
            _small((RG_GROUPS, RG_GROUP_W, RG_BLOCK)),
            _small((RG_GROUPS, RG_GROUP_W, RG_BLOCK)),
            _small((1, RG_WIDTH)),
            _small((1, RG_WIDTH)),
            _small((1, RG_WIDTH)),
            _small((2, HG_WIDTH)),
            _small((1, HG_DK)),
            _resident((RG_WIDTH, d)),
            _resident((HG_WIDTH, d)),
            _resident((d, d)),
        ],
        out_specs=pl.BlockSpec((nb, tt, d), x_map),
        out_shape=jax.ShapeDtypeStruct((nb, seq, d), _F32),
        scratch_shapes=[
            pltpu.VMEM(w_in_f.shape, _BF),
            pltpu.VMEM((2, d, PROJ_COLS), _F32),
            pltpu.SemaphoreType.DMA((2,)),
            pltpu.VMEM((RG_GROUPS, RG_GROUP_W, 2 * RG_GROUP_W), _BF),
            pltpu.VMEM((nb, tt + SUBLANES, RG_WIDTH), _F32),
            pltpu.VMEM((nb, 1, RG_WIDTH), _F32),
            pltpu.VMEM((nb, HG_HEADS, HG_DK, HG_DK), _F32),
        ],
        compiler_params=pltpu.CompilerParams(
            dimension_semantics=("arbitrary",), vmem_limit_bytes=VMEM_LIMIT_BYTES),
    )
    mixer_args = (
        meta_tile, x, row(norm1_g[0]), w_in_f, conv_w[0].astype(_F32), row(conv_b[0]),
        stack_blocks(rg_wa[0]), stack_blocks(rg_wx[0]),
        row(rg_ba[0]), row(rg_bx[0]), row(rg_lambda[0]), hg_lb_logits.astype(_F32),
        row(hg_norm_g[0]), w_proj_a[0].astype(_F32), w_proj_b[0].astype(_F32),
        w_out[0].astype(_F32))
    lower_bound = jax.nn.softmax(hg_lb_logits.astype(_F32), axis=0)[0]
    fast_ok = jnp.min(lower_bound) >= 2.0 ** (-FAST_LOG2_RANGE / FAST_BLOCK)
    h_mid = lax.cond(fast_ok, make_mixer(True), make_mixer(False), *mixer_args)

    n_rows = nb * seq
    ffn = pl.pallas_call(
        _ffn_kernel,
        name="ffn",
        grid=(n_rows // FFN_ROWS,),
        in_specs=[
            pl.BlockSpec((FFN_ROWS, d), lambda i: (i, 0)),
            _small((1, d)),
            _resident((d, 2 * D_FF)),
            _resident((D_FF, d)),
            _small((1, d)),
        ],
        out_specs=pl.BlockSpec((FFN_ROWS, d), lambda i: (i, 0)),
        out_shape=jax.ShapeDtypeStruct((n_rows, d), _F32),
        compiler_params=pltpu.CompilerParams(
            dimension_semantics=("arbitrary",), vmem_limit_bytes=VMEM_LIMIT_BYTES),
    )
    out = ffn(h_mid.reshape(n_rows, d), row(norm2_g[0]), w_ffn_in[0].astype(_F32),
              w_ffn_down[0].astype(_F32), row(norm_f_g))
    return out.reshape(nb, seq, d)
```

```python
import functools
import math

import jax
import jax.numpy as jnp
from jax import lax
from jax.experimental import pallas as pl
from jax.experimental.pallas import tpu as pltpu

D_MODEL = 1024
N_META = 16
RG_WIDTH = 1280
RG_BLOCKS = 16
RG_BLOCK = 80
RG_GROUPS = 2
RG_GROUP_W = RG_WIDTH // RG_GROUPS
RG_CONV = 4
RG_C = 8.0
HG_HEADS = 8
HG_DK = 128
HG_WIDTH = 1024
D_FF = 2816
NORM_EPS = 1e-6
LOG2_E = math.log2(math.e)

OFF_AX, OFF_AG, OFF_Q, OFF_F, OFF_I, OFF_G, OFF_GA, OFF_GB = (
    0, 1280, 2560, 3584, 4608, 5632, 6656, 7680)

SUBLANES = 8
TIME_TILE = 128
CHUNK = 64
FAST_BLOCK = 16
FAST_LOG2_RANGE = 96.0
assert FAST_BLOCK % (2 * SUBLANES) == 0 and CHUNK % FAST_BLOCK == 0
PROJ_COLS = 512
FFN_ROWS = 512
FFN_COLS = 1408
VMEM_LIMIT_BYTES = 60000 * 1024

_BF = jnp.bfloat16
_F32 = jnp.float32


def _dot(a, b):
    return jnp.dot(a, b, preferred_element_type=_F32)


def _dot_nt(a, b):
    return lax.dot_general(a, b, (((1,), (1,)), ((), ())), preferred_element_type=_F32)


def _dot_tn(a, b):
    return lax.dot_general(a, b, (((0,), (0,)), ((), ())), preferred_element_type=_F32)


def _rmsnorm(x, g):
    ms = jnp.mean(x * x, axis=-1, keepdims=True)
    return x * lax.rsqrt(ms + NORM_EPS) * g


def _sigmoid(x):
    return 0.5 * jnp.tanh(0.5 * x) + 0.5


def _silu(x):
    hx = 0.5 * x
    return hx * jnp.tanh(hx) + hx


def _sqrt_nonneg(x):
    return jnp.where(x > 0.0, x * lax.rsqrt(x), 0.0)


def _gelu_tanh(x):
    c = math.sqrt(2.0 / math.pi)
    return x * (0.5 * (1.0 + jnp.tanh(c * (x + 0.044715 * (x * x * x)))))


def _level_masks():
    t = lax.broadcasted_iota(jnp.int32, (CHUNK, CHUNK), 0)
    s = lax.broadcasted_iota(jnp.int32, (CHUNK, CHUNK), 1)
    masks = {0: t == s}
    h = 1
    while h < CHUNK:
        sh = int(math.log2(2 * h))
        masks[h] = ((t >> sh) == (s >> sh)) & ((t & h) != 0) & ((s & h) == 0)
        h *= 2
    groups = range(0, CHUNK, SUBLANES)
    out = {h: [jnp.where(m[r:r + SUBLANES], 1.0, 0.0).astype(_F32) for r in groups]
           for h, m in masks.items()}
    in_block = ((t // FAST_BLOCK) == (s // FAST_BLOCK)) & (s <= t)
    out["block"] = [in_block[r:r + SUBLANES] for r in groups]
    return out


def _hgrn_operands(g2, q, k, fast):
    w = g2.shape[-1]
    nv = CHUNK // SUBLANES
    sub = lax.broadcasted_iota(jnp.int32, (SUBLANES, w), 0)
    qs = [q[SUBLANES * j:SUBLANES * (j + 1)] for j in range(nv)]
    ks = [k[SUBLANES * j:SUBLANES * (j + 1)] for j in range(nv)]
    bcs = []
    run = None
    for j in range(nv):
        x = g2[SUBLANES * j:SUBLANES * (j + 1)]
        for s in (1, 2, 4):
            x = x + jnp.where(sub >= s, pltpu.roll(x, s, 0), 0.0)
        if run is not None:
            x = x + run
        bcs.append(x)
        run = x[SUBLANES - 1:SUBLANES]
    b_last = run

    def small_ref_row(h, j):
        x = bcs[j]
        rows = [jnp.broadcast_to(x[r:r + 1], x.shape)
                for r in range(h - 1, SUBLANES, 2 * h)]
        out = rows[-1]
        for idx in range(len(rows) - 2, -1, -1):
            out = jnp.where(sub < 2 * h * (idx + 1), rows[idx], out)
        return out

    levels = []
    h = CHUNK // 2
    while h >= (FAST_BLOCK if fast else 1):
        parts = []
        for j in range(nv):
            if h >= SUBLANES:
                r = (SUBLANES * j) // (2 * h) * (2 * h) + h - 1
                ref = bcs[r // SUBLANES][SUBLANES - 1:SUBLANES]
                if (SUBLANES * j) & h:
                    parts.append(qs[j] * jnp.exp2(bcs[j] - ref))
                else:
                    parts.append(ks[j] * jnp.exp2(ref - bcs[j]))
            else:
                base = jnp.where((sub & h) != 0, qs[j], ks[j])
                parts.append(base * jnp.exp2(-jnp.abs(bcs[j] - small_ref_row(h, j))))
        levels.append((h, jnp.concatenate(parts, axis=0).astype(_BF)))
        h //= 2
    if fast:
        pq, pk = [], []
        groups_per_block = FAST_BLOCK // SUBLANES
        for j in range(nv):
            mid = (j // groups_per_block) * groups_per_block + groups_per_block // 2 - 1
            e = bcs[j] - bcs[mid][SUBLANES - 1:SUBLANES]
            pq.append(qs[j] * jnp.exp2(e))
            pk.append(ks[j] * jnp.exp2(-e))
        q_bf = jnp.concatenate(pq, axis=0).astype(_BF)
        k_bf = jnp.concatenate(pk, axis=0).astype(_BF)
    else:
        q_bf = q.astype(_BF)
        k_bf = k.astype(_BF)
    q_dec = jnp.concatenate([qs[j] * jnp.exp2(bcs[j]) for j in range(nv)],
                            axis=0).astype(_BF)
    k_dec = jnp.concatenate([ks[j] * jnp.exp2(b_last - bcs[j]) for j in range(nv)],
                            axis=0).astype(_BF)
    s_dec = jnp.exp2(b_last)
    return levels, q_bf, k_bf, q_dec, k_dec, s_dec


def _hgrn_score_products(operands, masks, tick, fast):
    levels, q_bf, k_bf, q_dec, k_dec, s_dec = operands
    nv = CHUNK // SUBLANES
    all_scores = []
    for hd in range(HG_HEADS):
        sl = slice(hd * HG_DK, (hd + 1) * HG_DK)
        diag = _dot_nt(q_bf[:, sl], k_bf[:, sl])
        if fast:
            acc = [jnp.where(masks["block"][j], diag[SUBLANES * j:SUBLANES * (j + 1)], 0.0)
                   for j in range(nv)]
        else:
            acc = [diag[SUBLANES * j:SUBLANES * (j + 1)] * masks[0][j] for j in range(nv)]
        for h, x in levels:
            prod = _dot_nt(x[:, sl], x[:, sl])
            for j in range(nv):
                if h >= SUBLANES and not (SUBLANES * j) & h:
                    continue
                acc[j] = acc[j] + prod[SUBLANES * j:SUBLANES * (j + 1)] * masks[h][j]
        all_scores.append(jnp.concatenate(acc, axis=0).astype(_BF))
        if hd % 4 == 3:
            tick()
    return all_scores, q_dec, k_dec, s_dec


def _hgrn_apply(phase1, v_bf, st):
    all_scores, q_dec, k_dec, s_dec = phase1
    outs, new_st = [], []
    for hd in range(HG_HEADS):
        sl = slice(hd * HG_DK, (hd + 1) * HG_DK)
        o = _dot(all_scores[hd], v_bf[:, sl])
        o = o + _dot_nt(q_dec[:, sl], st[hd].astype(_BF))
        new_st.append(s_dec[:, sl] * st[hd] + _dot_tn(v_bf[:, sl], k_dec[:, sl]))
        outs.append(o)
    return jnp.concatenate(outs, axis=-1), new_st


def _projection_chunks():
    plan = (("ax", OFF_AX, RG_WIDTH), ("f", OFF_F, HG_WIDTH), ("q", OFF_Q, HG_WIDTH),
            ("v", OFF_I, HG_WIDTH), ("ag", OFF_AG, RG_WIDTH), ("ga", OFF_GA, D_MODEL),
            ("g", OFF_G, HG_WIDTH), ("gb", OFF_GB, D_MODEL))
    chunks = []
    for name, off, width in plan:
        for c in range(0, width, PROJ_COLS):
            chunks.append((name, off + c, min(PROJ_COLS, width - c)))
    return chunks


def _mixer_kernel(meta_ref, x_ref, n1g_ref, w_in_hbm, convw_ref, convb_ref, wa_ref, wx_ref,
                  ba_ref, bx_ref, lam_ref, lbl_ref, hgn_ref, wpa_ref, wpb_ref, wout_ref,
                  o_ref, w_in_ref, w_stage, w_sems, wg_buf, cbuf, hcar, st_ref, *, nb, tt,
                  fast):
    i = pl.program_id(0)
    rows = nb * tt
    is_meta = i == 0

    @pl.when(is_meta)
    def _():
        n_chunks = w_in_ref.shape[1] // PROJ_COLS

        def w_copy(c):
            return pltpu.make_async_copy(w_in_hbm.at[:, pl.ds(c * PROJ_COLS, PROJ_COLS)],
                                         w_stage.at[c % 2], w_sems.at[c % 2])

        w_copy(0).start()
        for c in range(n_chunks):
            if c + 1 < n_chunks:
                w_copy(c + 1).start()
            w_copy(c).wait()
            w_in_ref[:, c * PROJ_COLS:(c + 1) * PROJ_COLS] = w_stage[c % 2].astype(_BF)
        cbuf[...] = jnp.zeros_like(cbuf)
        hcar[...] = jnp.zeros_like(hcar)
        st_ref[...] = jnp.zeros_like(st_ref)
        kk = lax.broadcasted_iota(jnp.int32, (RG_BLOCK, RG_GROUP_W), 0)
        cc = lax.broadcasted_iota(jnp.int32, (RG_BLOCK, RG_GROUP_W), 1)
        tile_cols = functools.reduce(
            jnp.logical_or, [cc == kk + RG_BLOCK * n for n in range(RG_GROUP_W // RG_BLOCK)])
        tiling = jnp.where(tile_cols, 1.0, 0.0).astype(_BF)
        ri = lax.broadcasted_iota(jnp.int32, (RG_GROUP_W, RG_GROUP_W), 0)
        ci = lax.broadcasted_iota(jnp.int32, (RG_GROUP_W, RG_GROUP_W), 1)
        same_block = functools.reduce(
            jnp.logical_or,
            [jnp.logical_and(jnp.logical_and(ri >= RG_BLOCK * n, ri < RG_BLOCK * (n + 1)),
                             jnp.logical_and(ci >= RG_BLOCK * n, ci < RG_BLOCK * (n + 1)))
             for n in range(RG_GROUP_W // RG_BLOCK)])
        for gidx in range(RG_GROUPS):
            for col, w_ref in ((0, wa_ref), (RG_GROUP_W, wx_ref)):
                spread = _dot(w_ref[gidx].astype(_BF), tiling)
                wg_buf[gidx, :, col:col + RG_GROUP_W] = jnp.where(
                    same_block, spread, 0.0).astype(_BF)

    meta = meta_ref[...]
    h_in = jnp.concatenate([jnp.where(is_meta, meta, x_ref[b]) for b in range(nb)], axis=0)
    hn = _rmsnorm(h_in, n1g_ref[...]).astype(_BF)

    chunks = _projection_chunks()
    issued = {}
    cursor = iter(chunks)

    def project_next(n=1):
        for _ in range(n):
            nxt = next(cursor, None)
            if nxt is None:
                return
            name, off, width = nxt
            issued.setdefault(name, []).append(_dot(hn, w_in_ref[:, off:off + width]))

    def projected(name):
        total = sum(1 for c in chunks if c[0] == name)
        while len(issued.get(name, ())) < total:
            project_next()
        return jnp.concatenate(issued[name], axis=-1)

    a_x = projected("ax")

    convw = convw_ref[...]
    xc_parts = []
    for b in range(nb):
        cbuf[b, SUBLANES:, :] = a_x[b * tt:(b + 1) * tt]
        acc = convb_ref[...] + convw[RG_CONV - 1:RG_CONV] * cbuf[b, pl.ds(SUBLANES, tt), :]
        for j in range(RG_CONV - 1):
            off = SUBLANES - (RG_CONV - 1) + j
            acc = acc + convw[j:j + 1] * cbuf[b, pl.ds(off, tt), :]
        xc_parts.append(acc)
        cbuf[b, 0:SUBLANES, :] = cbuf[b, pl.ds(tt, SUBLANES), :]
    xc = jnp.concatenate(xc_parts, axis=0)
    xc_bf = xc.astype(_BF)
    z = -lam_ref[...]
    softplus = jnp.maximum(z, 0.0) + jnp.log1p(jnp.exp(-jnp.abs(z)))
    a_scale = (-0.5 * RG_C * LOG2_E) * softplus
    a_parts, u_parts = [], []
    for gidx in range(RG_GROUPS):
        lo = gidx * RG_GROUP_W
        cols = slice(lo, lo + RG_GROUP_W)
        zz = _dot(xc_bf[:, cols], wg_buf[gidx])
        r_tanh = jnp.tanh(0.5 * (zz[:, :RG_GROUP_W] + ba_ref[:, cols]))
        a_g = jnp.exp2(a_scale[:, cols] * r_tanh + a_scale[:, cols])
        i_g = _sigmoid(zz[:, RG_GROUP_W:] + bx_ref[:, cols])
        a_parts.append(a_g)
        u_parts.append(_sqrt_nonneg(1.0 - a_g * a_g) * (i_g * xc[:, cols]))
        project_next()
    a = jnp.concatenate(a_parts, axis=-1)
    u = jnp.concatenate(u_parts, axis=-1)
    t_loc = lax.broadcasted_iota(jnp.int32, (rows, 1), 0) % tt
    u = jnp.where(jnp.logical_or(i > 0, t_loc >= tt - N_META), u, 0.0)

    sub = lax.broadcasted_iota(jnp.int32, (SUBLANES, RG_WIDTH), 0)
    h_parts = []
    for b in range(nb):
        carry = hcar[b]
        for j in range(tt // SUBLANES):
            r0 = b * tt + j * SUBLANES
            av = a[r0:r0 + SUBLANES]
            hv = u[r0:r0 + SUBLANES]
            for s in (1, 2, 4):
                keep = sub >= s
                hv = hv + av * jnp.where(keep, pltpu.roll(hv, s, 0), 0.0)
                av = av * jnp.where(keep, pltpu.roll(av, s, 0), 1.0)
            hv = hv + av * carry
            carry = hv[SUBLANES - 1:SUBLANES]
            h_parts.append(hv)
            if j % 8 == 7:
                project_next()
        hcar[b] = carry
    h_rg = jnp.concatenate(h_parts, axis=0)

    l0 = lbl_ref[0:1, :]
    l1 = lbl_ref[1:2, :]
    lmax = jnp.maximum(l0, l1)
    e0 = jnp.exp(l0 - lmax)
    e1 = jnp.exp(l1 - lmax)
    lb = e0 / (e0 + e1)
    f_sig = _sigmoid(projected("f"))
    log2_f = jnp.log(lb + (1.0 - lb) * f_sig) * LOG2_E
    k_in = (1.0 - lb) * (1.0 - f_sig)
    q_in = _silu(projected("q"))
    project_next()
    v_in = projected("v").astype(_BF)

    masks = _level_masks()
    order = [(b, c) for c in range(tt // CHUNK) for b in range(nb)]
    st = {b: [st_ref[b, hd] for hd in range(HG_HEADS)] for b in range(nb)}
    o_chunks = {}

    def rows_of(b, c):
        r0 = b * tt + c * CHUNK
        return slice(r0, r0 + CHUNK)

    def apply_phase(b, c, phase1):
        o_chunks[(b, c)], st[b] = _hgrn_apply(phase1, v_in[rows_of(b, c)], st[b])

    prepared = []
    for b, c in order:
        sl = rows_of(b, c)
        prepared.append(_hgrn_operands(log2_f[sl], q_in[sl], k_in[sl], fast))
        project_next()
    pending = None
    for (b, c), operands in zip(order, prepared):
        phase1 = _hgrn_score_products(operands, masks, project_next, fast)
        if pending is not None:
            apply_phase(*pending)
        pending = (b, c, phase1)
    apply_phase(*pending)
    for b in range(nb):
        for hd in range(HG_HEADS):
            st_ref[b, hd] = st[b][hd]
    o_b = jnp.concatenate([o_chunks[(b, c)] for b in range(nb) for c in range(tt // CHUNK)],
                          axis=0)

    y_a = (h_rg * _gelu_tanh(projected("ag"))).astype(_BF)
    merged = _sigmoid(projected("ga")) * _dot(y_a, wpa_ref[...].astype(_BF))
    g_act = _silu(projected("g"))
    hgn = hgn_ref[...]
    yb_parts = []
    for hd in range(HG_HEADS):
        sl = slice(hd * HG_DK, (hd + 1) * HG_DK)
        yb_parts.append((_rmsnorm(o_b[:, sl], hgn) * g_act[:, sl]).astype(_BF))
    y_b = jnp.concatenate(yb_parts, axis=-1)
    merged = merged + _sigmoid(projected("gb")) * _dot(y_b, wpb_ref[...].astype(_BF))

    out = h_in + _dot(merged.astype(_BF), wout_ref[...].astype(_BF))
    for b in range(nb):
        o_ref[b] = out[b * tt:(b + 1) * tt]


def _ffn_kernel(h_ref, n2g_ref, wfi_ref, wfd_ref, nfg_ref, o_ref):
    h = h_ref[...]
    hn = _rmsnorm(h, n2g_ref[...]).astype(_BF)
    acc = h
    for j in range(D_FF // FFN_COLS):
        lo = j * FFN_COLS
        gcol = _dot(hn, wfi_ref[:, lo:lo + FFN_COLS].astype(_BF))
        ucol = _dot(hn, wfi_ref[:, D_FF + lo:D_FF + lo + FFN_COLS].astype(_BF))
        act = (_silu(gcol) * ucol).astype(_BF)
        acc = acc + _dot(act, wfd_ref[lo:lo + FFN_COLS, :].astype(_BF))
    o_ref[...] = _rmsnorm(acc, nfg_ref[...])


def _resident(shape):
    nd = len(shape)
    return pl.BlockSpec(shape, lambda *_: (0,) * nd, pipeline_mode=pl.Buffered(1))


def _small(shape):
    nd = len(shape)
    return pl.BlockSpec(shape, lambda *_: (0,) * nd)


def kernel(x, meta_tokens, norm1_g, w_in, conv_w, conv_b, rg_wa, rg_ba, rg_wx, rg_bx,
           rg_lambda, hg_lb_logits, hg_norm_g, w_proj_a, w_proj_b, w_out, norm2_g,
           w_ffn_in, w_ffn_down, norm_f_g):
    nb, seq, d = x.shape
    assert d == D_MODEL and seq % TIME_TILE == 0 and (nb * seq) % FFN_ROWS == 0
    assert w_in.shape[0] == 1, "single-layer block"
    tt = TIME_TILE
    n_steps = 1 + seq // tt

    meta_tile = jnp.zeros((tt, d), x.dtype).at[tt - N_META:].set(meta_tokens.astype(x.dtype))
    w_in_f = w_in[0].astype(_F32)
    assert w_in_f.shape[1] % PROJ_COLS == 0
    stack_blocks = lambda w: w.reshape(RG_GROUPS, RG_GROUP_W, RG_BLOCK).astype(_F32)
    row = lambda v: v.reshape(1, -1).astype(_F32)

    x_map = lambda i: (0, jnp.maximum(i - 1, 0), 0)
    make_mixer = lambda fast: pl.pallas_call(
        functools.partial(_mixer_kernel, nb=nb, tt=tt, fast=fast),
        name="mixer_fast" if fast else "mixer",
        grid=(n_steps,),
        in_specs=[
            _resident((tt, d)),
            pl.BlockSpec((nb, tt, d), x_map),
            _small((1, d)),
            pl.BlockSpec(memory_space=pl.ANY),
            _small((RG_CONV, RG_WIDTH)),
            _small((1, RG_WIDTH)),
            _small((RG_GROUPS, RG_GROUP_W, RG_BLOCK)),
            _small((RG_GROUPS, RG_GROUP_W, RG_BLOCK)),
            _small((1, RG_WIDTH)),
            _small((1, RG_WIDTH)),
            _small((1, RG_WIDTH)),
            _small((2, HG_WIDTH)),
            _small((1, HG_DK)),
            _resident((RG_WIDTH, d)),
            _resident((HG_WIDTH, d)),
            _resident((d, d)),
        ],
        out_specs=pl.BlockSpec((nb, tt, d), x_map),
        out_shape=jax.ShapeDtypeStruct((nb, seq, d), _F32),
        scratch_shapes=[
            pltpu.VMEM(w_in_f.shape, _BF),
            pltpu.VMEM((2, d, PROJ_COLS), _F32),
            pltpu.SemaphoreType.DMA((2,)),
            pltpu.VMEM((RG_GROUPS, RG_GROUP_W, 2 * RG_GROUP_W), _BF),
            pltpu.VMEM((nb, tt + SUBLANES, RG_WIDTH), _F32),
            pltpu.VMEM((nb, 1, RG_WIDTH), _F32),
            pltpu.VMEM((nb, HG_HEADS, HG_DK, HG_DK), _F32),
        ],
        compiler_params=pltpu.CompilerParams(
            dimension_semantics=("arbitrary",), vmem_limit_bytes=VMEM_LIMIT_BYTES),
    )
    mixer_args = (
        meta_tile, x, row(norm1_g[0]), w_in_f, conv_w[0].astype(_F32), row(conv_b[0]),
        stack_blocks(rg_wa[0]), stack_blocks(rg_wx[0]),
        row(rg_ba[0]), row(rg_bx[0]), row(rg_lambda[0]), hg_lb_logits.astype(_F32),
        row(hg_norm_g[0]), w_proj_a[0].astype(_F32), w_proj_b[0].astype(_F32),
        w_out[0].astype(_F32))
    lower_bound = jax.nn.softmax(hg_lb_logits.astype(_F32), axis=0)[0]
    fast_ok = jnp.min(lower_bound) >= 2.0 ** (-FAST_LOG2_RANGE / (FAST_BLOCK // 2))
    h_mid = lax.cond(fast_ok, make_mixer(True), make_mixer(False), *mixer_args)

    n_rows = nb * seq
    ffn = pl.pallas_call(
        _ffn_kernel,
        name="ffn",
        grid=(n_rows // FFN_ROWS,),
        in_specs=[
            pl.BlockSpec((FFN_ROWS, d), lambda i: (i, 0)),
            _small((1, d)),
            _resident((d, 2 * D_FF)),
            _resident((D_FF, d)),
            _small((1, d)),
        ],
        out_specs=pl.BlockSpec((FFN_ROWS, d), lambda i: (i, 0)),
        out_shape=jax.ShapeDtypeStruct((n_rows, d), _F32),
        compiler_params=pltpu.CompilerParams(
            dimension_semantics=("arbitrary",), vmem_limit_bytes=VMEM_LIMIT_BYTES),
    )
    out = ffn(h_mid.reshape(n_rows, d), row(norm2_g[0]), w_ffn_in[0].astype(_F32),
              w_ffn_down[0].astype(_F32), row(norm_f_g))
    return out.reshape(nb, seq, d)
```

```python
import functools
import math

import jax
import jax.numpy as jnp
from jax import lax
from jax.experimental import pallas as pl
from jax.experimental.pallas import tpu as pltpu

D_MODEL = 1024
N_META = 16
RG_WIDTH = 1280
RG_BLOCKS = 16
RG_BLOCK = 80
RG_GROUPS = 2
RG_GROUP_W = RG_WIDTH // RG_GROUPS
RG_CONV = 4
RG_C = 8.0
HG_HEADS = 8
HG_DK = 128
HG_WIDTH = 1024
D_FF = 2816
NORM_EPS = 1e-6
LOG2_E = math.log2(math.e)

OFF_AX, OFF_AG, OFF_Q, OFF_F, OFF_I, OFF_G, OFF_GA, OFF_GB = (
    0, 1280, 2560, 3584, 4608, 5632, 6656, 7680)

SUBLANES = 8
TIME_TILE = 128
CHUNK = 64
FAST_BLOCK = 16
FAST_LOG2_RANGE = 96.0
assert FAST_BLOCK % (2 * SUBLANES) == 0 and CHUNK % FAST_BLOCK == 0
PROJ_COLS = 512
FFN_ROWS = 512
MXU_TILE = 256
FFN_PASS_TILES = 6
VMEM_LIMIT_BYTES = 60000 * 1024

_BF = jnp.bfloat16
_F32 = jnp.float32


def _dot(a, b):
    return jnp.dot(a, b, preferred_element_type=_F32)


def _dot_nt(a, b):
    return lax.dot_general(a, b, (((1,), (1,)), ((), ())), preferred_element_type=_F32)


def _dot_tn(a, b):
    return lax.dot_general(a, b, (((0,), (0,)), ((), ())), preferred_element_type=_F32)


def _rmsnorm(x, g):
    ms = jnp.mean(x * x, axis=-1, keepdims=True)
    return x * lax.rsqrt(ms + NORM_EPS) * g


def _sigmoid(x):
    return 0.5 * jnp.tanh(0.5 * x) + 0.5


def _silu(x):
    hx = 0.5 * x
    return hx * jnp.tanh(hx) + hx


def _sqrt_nonneg(x):
    return jnp.where(x > 0.0, x * lax.rsqrt(x), 0.0)


def _gelu_tanh(x):
    c = math.sqrt(2.0 / math.pi)
    return x * (0.5 * (1.0 + jnp.tanh(c * (x + 0.044715 * (x * x * x)))))


def _level_masks():
    t = lax.broadcasted_iota(jnp.int32, (CHUNK, CHUNK), 0)
    s = lax.broadcasted_iota(jnp.int32, (CHUNK, CHUNK), 1)
    masks = {0: t == s}
    h = 1
    while h < CHUNK:
        sh = int(math.log2(2 * h))
        masks[h] = ((t >> sh) == (s >> sh)) & ((t & h) != 0) & ((s & h) == 0)
        h *= 2
    groups = range(0, CHUNK, SUBLANES)
    out = {h: [jnp.where(m[r:r + SUBLANES], 1.0, 0.0).astype(_F32) for r in groups]
           for h, m in masks.items()}
    in_block = ((t // FAST_BLOCK) == (s // FAST_BLOCK)) & (s <= t)
    out["block"] = [in_block[r:r + SUBLANES] for r in groups]
    return out


def _hgrn_operands(g2, q, k, fast):
    w = g2.shape[-1]
    nv = CHUNK // SUBLANES
    sub = lax.broadcasted_iota(jnp.int32, (SUBLANES, w), 0)
    qs = [q[SUBLANES * j:SUBLANES * (j + 1)] for j in range(nv)]
    ks = [k[SUBLANES * j:SUBLANES * (j + 1)] for j in range(nv)]
    bcs = []
    run = None
    for j in range(nv):
        x = g2[SUBLANES * j:SUBLANES * (j + 1)]
        for s in (1, 2, 4):
            x = x + jnp.where(sub >= s, pltpu.roll(x, s, 0), 0.0)
        if run is not None:
            x = x + run
        bcs.append(x)
        run = x[SUBLANES - 1:SUBLANES]
    b_last = run

    def small_ref_row(h, j):
        x = bcs[j]
        rows = [jnp.broadcast_to(x[r:r + 1], x.shape)
                for r in range(h - 1, SUBLANES, 2 * h)]
        out = rows[-1]
        for idx in range(len(rows) - 2, -1, -1):
            out = jnp.where(sub < 2 * h * (idx + 1), rows[idx], out)
        return out

    levels = []
    h = CHUNK // 2
    while h >= (FAST_BLOCK if fast else 1):
        parts = []
        for j in range(nv):
            if h >= SUBLANES:
                r = (SUBLANES * j) // (2 * h) * (2 * h) + h - 1
                ref = bcs[r // SUBLANES][SUBLANES - 1:SUBLANES]
                if (SUBLANES * j) & h:
                    parts.append(qs[j] * jnp.exp2(bcs[j] - ref))
                else:
                    parts.append(ks[j] * jnp.exp2(ref - bcs[j]))
            else:
                base = jnp.where((sub & h) != 0, qs[j], ks[j])
                parts.append(base * jnp.exp2(-jnp.abs(bcs[j] - small_ref_row(h, j))))
        levels.append((h, jnp.concatenate(parts, axis=0).astype(_BF)))
        h //= 2
    if fast:
        pq, pk = [], []
        groups_per_block = FAST_BLOCK // SUBLANES
        for j in range(nv):
            mid = (j // groups_per_block) * groups_per_block + groups_per_block // 2 - 1
            e = bcs[j] - bcs[mid][SUBLANES - 1:SUBLANES]
            pq.append(qs[j] * jnp.exp2(e))
            pk.append(ks[j] * jnp.exp2(-e))
        q_bf = jnp.concatenate(pq, axis=0).astype(_BF)
        k_bf = jnp.concatenate(pk, axis=0).astype(_BF)
    else:
        q_bf = q.astype(_BF)
        k_bf = k.astype(_BF)
    q_dec = jnp.concatenate([qs[j] * jnp.exp2(bcs[j]) for j in range(nv)],
                            axis=0).astype(_BF)
    k_dec = jnp.concatenate([ks[j] * jnp.exp2(b_last - bcs[j]) for j in range(nv)],
                            axis=0).astype(_BF)
    s_dec = jnp.exp2(b_last)
    return levels, q_bf, k_bf, q_dec, k_dec, s_dec


def _hgrn_score_products(operands, masks, tick, fast):
    levels, q_bf, k_bf, q_dec, k_dec, s_dec = operands
    nv = CHUNK // SUBLANES
    all_scores = []
    for hd in range(HG_HEADS):
        sl = slice(hd * HG_DK, (hd + 1) * HG_DK)
        diag = _dot_nt(q_bf[:, sl], k_bf[:, sl])
        if fast:
            acc = [jnp.where(masks["block"][j], diag[SUBLANES * j:SUBLANES * (j + 1)], 0.0)
                   for j in range(nv)]
        else:
            acc = [diag[SUBLANES * j:SUBLANES * (j + 1)] * masks[0][j] for j in range(nv)]
        for h, x in levels:
            prod = _dot_nt(x[:, sl], x[:, sl])
            for j in range(nv):
                if h >= SUBLANES and not (SUBLANES * j) & h:
                    continue
                acc[j] = acc[j] + prod[SUBLANES * j:SUBLANES * (j + 1)] * masks[h][j]
        all_scores.append(jnp.concatenate(acc, axis=0).astype(_BF))
        if hd % 4 == 3:
            tick()
    return all_scores, q_dec, k_dec, s_dec


def _hgrn_apply(phase1, v_bf, st):
    all_scores, q_dec, k_dec, s_dec = phase1
    outs, new_st = [], []
    for hd in range(HG_HEADS):
        sl = slice(hd * HG_DK, (hd + 1) * HG_DK)
        o = _dot(all_scores[hd], v_bf[:, sl])
        o = o + _dot_nt(q_dec[:, sl], st[hd].astype(_BF))
        new_st.append(s_dec[:, sl] * st[hd] + _dot_tn(v_bf[:, sl], k_dec[:, sl]))
        outs.append(o)
    return jnp.concatenate(outs, axis=-1), new_st


def _projection_chunks():
    plan = (("ax", OFF_AX, RG_WIDTH), ("f", OFF_F, HG_WIDTH), ("q", OFF_Q, HG_WIDTH),
            ("v", OFF_I, HG_WIDTH), ("ag", OFF_AG, RG_WIDTH), ("ga", OFF_GA, D_MODEL),
            ("g", OFF_G, HG_WIDTH), ("gb", OFF_GB, D_MODEL))
    chunks = []
    for name, off, width in plan:
        for c in range(0, width, PROJ_COLS):
            chunks.append((name, off + c, min(PROJ_COLS, width - c)))
    return chunks


def _stream_chunks(shapes):
    plan = []
    for m, (n_rows, n_cols) in enumerate(shapes):
        parts = -(-n_rows // D_MODEL)
        assert n_rows % (parts * 2 * SUBLANES) == 0 and n_cols % PROJ_COLS == 0
        for r in range(0, n_rows, n_rows // parts):
            for c in range(0, n_cols, PROJ_COLS):
                plan.append((m, r, n_rows // parts, c))
    return plan


def _mixer_kernel(meta_ref, x_ref, n1g_ref, w_in_hbm, convw_ref, convb_ref, wa_ref, wx_ref,
                  ba_ref, bx_ref, lam_ref, lbl_ref, hgn_ref, wpa_hbm, wpb_hbm, wout_hbm,
                  o_ref, w_in_ref, wpa_ref, wpb_ref, wout_ref, w_stage, w_sems, wg_buf, cbuf,
                  hcar, st_ref, *, nb, tt, fast):
    i = pl.program_id(0)
    rows = nb * tt
    is_meta = i == 0

    @pl.when(is_meta)
    def _():
        sources = (w_in_hbm, wpa_hbm, wpb_hbm, wout_hbm)
        kept = (w_in_ref, wpa_ref, wpb_ref, wout_ref)
        plan = _stream_chunks([ref.shape for ref in kept])

        def w_copy(n):
            m, r0, n_rows, c0 = plan[n]
            return pltpu.make_async_copy(
                sources[m].at[pl.ds(r0, n_rows), pl.ds(c0, PROJ_COLS)],
                w_stage.at[n % 2, pl.ds(0, n_rows), :], w_sems.at[n % 2])

        w_copy(0).start()
        for n, (m, r0, n_rows, c0) in enumerate(plan):
            if n + 1 < len(plan):
                w_copy(n + 1).start()
            w_copy(n).wait()
            kept[m][r0:r0 + n_rows, c0:c0 + PROJ_COLS] = w_stage[n % 2, 0:n_rows, :].astype(_BF)
        cbuf[...] = jnp.zeros_like(cbuf)
        hcar[...] = jnp.zeros_like(hcar)
        st_ref[...] = jnp.zeros_like(st_ref)
        kk = lax.broadcasted_iota(jnp.int32, (RG_BLOCK, RG_GROUP_W), 0)
        cc = lax.broadcasted_iota(jnp.int32, (RG_BLOCK, RG_GROUP_W), 1)
        tile_cols = functools.reduce(
            jnp.logical_or, [cc == kk + RG_BLOCK * n for n in range(RG_GROUP_W // RG_BLOCK)])
        tiling = jnp.where(tile_cols, 1.0, 0.0).astype(_BF)
        ri = lax.broadcasted_iota(jnp.int32, (RG_GROUP_W, RG_GROUP_W), 0)
        ci = lax.broadcasted_iota(jnp.int32, (RG_GROUP_W, RG_GROUP_W), 1)
        same_block = functools.reduce(
            jnp.logical_or,
            [jnp.logical_and(jnp.logical_and(ri >= RG_BLOCK * n, ri < RG_BLOCK * (n + 1)),
                             jnp.logical_and(ci >= RG_BLOCK * n, ci < RG_BLOCK * (n + 1)))
             for n in range(RG_GROUP_W // RG_BLOCK)])
        for gidx in range(RG_GROUPS):
            for col, w_ref in ((0, wa_ref), (RG_GROUP_W, wx_ref)):
                spread = _dot(w_ref[gidx].astype(_BF), tiling)
                wg_buf[gidx, :, col:col + RG_GROUP_W] = jnp.where(
                    same_block, spread, 0.0).astype(_BF)

    meta = meta_ref[...]
    h_in = jnp.concatenate([jnp.where(is_meta, meta, x_ref[b]) for b in range(nb)], axis=0)
    hn = _rmsnorm(h_in, n1g_ref[...]).astype(_BF)

    chunks = _projection_chunks()
    issued = {}
    cursor = iter(chunks)

    def project_next(n=1):
        for _ in range(n):
            nxt = next(cursor, None)
            if nxt is None:
                return
            name, off, width = nxt
            issued.setdefault(name, []).append(_dot(hn, w_in_ref[:, off:off + width]))

    def projected(name):
        total = sum(1 for c in chunks if c[0] == name)
        while len(issued.get(name, ())) < total:
            project_next()
        return jnp.concatenate(issued[name], axis=-1)

    a_x = projected("ax")

    convw = convw_ref[...]
    xc_parts = []
    for b in range(nb):
        cbuf[b, SUBLANES:, :] = a_x[b * tt:(b + 1) * tt]
        acc = convb_ref[...] + convw[RG_CONV - 1:RG_CONV] * cbuf[b, pl.ds(SUBLANES, tt), :]
        for j in range(RG_CONV - 1):
            off = SUBLANES - (RG_CONV - 1) + j
            acc = acc + convw[j:j + 1] * cbuf[b, pl.ds(off, tt), :]
        xc_parts.append(acc)
        cbuf[b, 0:SUBLANES, :] = cbuf[b, pl.ds(tt, SUBLANES), :]
    xc = jnp.concatenate(xc_parts, axis=0)
    xc_bf = xc.astype(_BF)
    z = -lam_ref[...]
    softplus = jnp.maximum(z, 0.0) + jnp.log1p(jnp.exp(-jnp.abs(z)))
    a_scale = (-0.5 * RG_C * LOG2_E) * softplus
    a_parts, u_parts = [], []
    for gidx in range(RG_GROUPS):
        lo = gidx * RG_GROUP_W
        cols = slice(lo, lo + RG_GROUP_W)
        zz = _dot(xc_bf[:, cols], wg_buf[gidx])
        r_tanh = jnp.tanh(0.5 * (zz[:, :RG_GROUP_W] + ba_ref[:, cols]))
        a_g = jnp.exp2(a_scale[:, cols] * r_tanh + a_scale[:, cols])
        i_g = _sigmoid(zz[:, RG_GROUP_W:] + bx_ref[:, cols])
        a_parts.append(a_g)
        u_parts.append(_sqrt_nonneg(1.0 - a_g * a_g) * (i_g * xc[:, cols]))
        project_next()
    a = jnp.concatenate(a_parts, axis=-1)
    u = jnp.concatenate(u_parts, axis=-1)
    t_loc = lax.broadcasted_iota(jnp.int32, (rows, 1), 0) % tt
    u = jnp.where(jnp.logical_or(i > 0, t_loc >= tt - N_META), u, 0.0)

    sub = lax.broadcasted_iota(jnp.int32, (SUBLANES, RG_WIDTH), 0)
    h_parts = []
    for b in range(nb):
        carry = hcar[b]
        for j in range(tt // SUBLANES):
            r0 = b * tt + j * SUBLANES
            av = a[r0:r0 + SUBLANES]
            hv = u[r0:r0 + SUBLANES]
            for s in (1, 2, 4):
                keep = sub >= s
                hv = hv + av * jnp.where(keep, pltpu.roll(hv, s, 0), 0.0)
                av = av * jnp.where(keep, pltpu.roll(av, s, 0), 1.0)
            hv = hv + av * carry
            carry = hv[SUBLANES - 1:SUBLANES]
            h_parts.append(hv)
            if j % 8 == 7:
                project_next()
        hcar[b] = carry
    h_rg = jnp.concatenate(h_parts, axis=0)

    l0 = lbl_ref[0:1, :]
    l1 = lbl_ref[1:2, :]
    lmax = jnp.maximum(l0, l1)
    e0 = jnp.exp(l0 - lmax)
    e1 = jnp.exp(l1 - lmax)
    lb = e0 / (e0 + e1)
    f_sig = _sigmoid(projected("f"))
    log2_f = jnp.log(lb + (1.0 - lb) * f_sig) * LOG2_E
    k_in = (1.0 - lb) * (1.0 - f_sig)
    q_in = _silu(projected("q"))
    project_next()
    v_in = projected("v").astype(_BF)

    masks = _level_masks()
    order = [(b, c) for c in range(tt // CHUNK) for b in range(nb)]
    st = {b: [st_ref[b, hd] for hd in range(HG_HEADS)] for b in range(nb)}
    o_chunks = {}

    def rows_of(b, c):
        r0 = b * tt + c * CHUNK
        return slice(r0, r0 + CHUNK)

    def apply_phase(b, c, phase1):
        o_chunks[(b, c)], st[b] = _hgrn_apply(phase1, v_in[rows_of(b, c)], st[b])

    prepared = []
    for b, c in order:
        sl = rows_of(b, c)
        prepared.append(_hgrn_operands(log2_f[sl], q_in[sl], k_in[sl], fast))
        project_next()
    pending = None
    for (b, c), operands in zip(order, prepared):
        phase1 = _hgrn_score_products(operands, masks, project_next, fast)
        if pending is not None:
            apply_phase(*pending)
        pending = (b, c, phase1)
    apply_phase(*pending)
    for b in range(nb):
        for hd in range(HG_HEADS):
            st_ref[b, hd] = st[b][hd]
    o_b = jnp.concatenate([o_chunks[(b, c)] for b in range(nb) for c in range(tt // CHUNK)],
                          axis=0)

    y_a = (h_rg * _gelu_tanh(projected("ag"))).astype(_BF)
    merged = _sigmoid(projected("ga")) * _dot(y_a, wpa_ref[...])
    g_act = _silu(projected("g"))
    hgn = hgn_ref[...]
    yb_parts = []
    for hd in range(HG_HEADS):
        sl = slice(hd * HG_DK, (hd + 1) * HG_DK)
        yb_parts.append((_rmsnorm(o_b[:, sl], hgn) * g_act[:, sl]).astype(_BF))
    y_b = jnp.concatenate(yb_parts, axis=-1)
    merged = merged + _sigmoid(projected("gb")) * _dot(y_b, wpb_ref[...])

    out = h_in + _dot(merged.astype(_BF), wout_ref[...])
    for b in range(nb):
        o_ref[b] = out[b * tt:(b + 1) * tt]


def _ffn_kernel(h_ref, n2g_ref, wfi_ref, wfd_ref, nfg_ref, o_ref):
    h = h_ref[...]
    hn = _rmsnorm(h, n2g_ref[...]).astype(_BF)
    acc = h
    assert D_FF % MXU_TILE == 0
    step = FFN_PASS_TILES * MXU_TILE
    for lo in range(0, D_FF, step):
        width = min(step, D_FF - lo)
        gcol = _dot(hn, wfi_ref[:, lo:lo + width].astype(_BF))
        ucol = _dot(hn, wfi_ref[:, D_FF + lo:D_FF + lo + width].astype(_BF))
        act = (_silu(gcol) * ucol).astype(_BF)
        acc = acc + _dot(act, wfd_ref[lo:lo + width, :].astype(_BF))
    o_ref[...] = _rmsnorm(acc, nfg_ref[...])


def _resident(shape):
    nd = len(shape)
    return pl.BlockSpec(shape, lambda *_: (0,) * nd, pipeline_mode=pl.Buffered(1))


def _small(shape):
    nd = len(shape)
    return pl.BlockSpec(shape, lambda *_: (0,) * nd)


def kernel(x, meta_tokens, norm1_g, w_in, conv_w, conv_b, rg_wa, rg_ba, rg_wx, rg_bx,
           rg_lambda, hg_lb_logits, hg_norm_g, w_proj_a, w_proj_b, w_out, norm2_g,
           w_ffn_in, w_ffn_down, norm_f_g):
    nb, seq, d = x.shape
    assert d == D_MODEL and seq % TIME_TILE == 0 and (nb * seq) % FFN_ROWS == 0
    assert w_in.shape[0] == 1, "single-layer block"
    tt = TIME_TILE
    n_steps = 1 + seq // tt

    meta_tile = jnp.zeros((tt, d), x.dtype).at[tt - N_META:].set(meta_tokens.astype(x.dtype))
    w_in_f = w_in[0].astype(_F32)
    assert w_in_f.shape[1] % PROJ_COLS == 0
    stack_blocks = lambda w: w.reshape(RG_GROUPS, RG_GROUP_W, RG_BLOCK).astype(_F32)
    row = lambda v: v.reshape(1, -1).astype(_F32)

    x_map = lambda i: (0, jnp.maximum(i - 1, 0), 0)
    make_mixer = lambda fast: pl.pallas_call(
        functools.partial(_mixer_kernel, nb=nb, tt=tt, fast=fast),
        name="mixer_fast" if fast else "mixer",
        grid=(n_steps,),
        in_specs=[
            _resident((tt, d)),
            pl.BlockSpec((nb, tt, d), x_map),
            _small((1, d)),
            pl.BlockSpec(memory_space=pl.ANY),
            _small((RG_CONV, RG_WIDTH)),
            _small((1, RG_WIDTH)),
            _small((RG_GROUPS, RG_GROUP_W, RG_BLOCK)),
            _small((RG_GROUPS, RG_GROUP_W, RG_BLOCK)),
            _small((1, RG_WIDTH)),
            _small((1, RG_WIDTH)),
            _small((1, RG_WIDTH)),
            _small((2, HG_WIDTH)),
            _small((1, HG_DK)),
            pl.BlockSpec(memory_space=pl.ANY),
            pl.BlockSpec(memory_space=pl.ANY),
            pl.BlockSpec(memory_space=pl.ANY),
        ],
        out_specs=pl.BlockSpec((nb, tt, d), x_map),
        out_shape=jax.ShapeDtypeStruct((nb, seq, d), _F32),
        scratch_shapes=[
            pltpu.VMEM(w_in_f.shape, _BF),
            pltpu.VMEM((RG_WIDTH, d), _BF),
            pltpu.VMEM((HG_WIDTH, d), _BF),
            pltpu.VMEM((d, d), _BF),
            pltpu.VMEM((2, d, PROJ_COLS), _F32),
            pltpu.SemaphoreType.DMA((2,)),
            pltpu.VMEM((RG_GROUPS, RG_GROUP_W, 2 * RG_GROUP_W), _BF),
            pltpu.VMEM((nb, tt + SUBLANES, RG_WIDTH), _F32),
            pltpu.VMEM((nb, 1, RG_WIDTH), _F32),
            pltpu.VMEM((nb, HG_HEADS, HG_DK, HG_DK), _F32),
        ],
        compiler_params=pltpu.CompilerParams(
            dimension_semantics=("arbitrary",), vmem_limit_bytes=VMEM_LIMIT_BYTES),
    )
    mixer_args = (
        meta_tile, x, row(norm1_g[0]), w_in_f, conv_w[0].astype(_F32), row(conv_b[0]),
        stack_blocks(rg_wa[0]), stack_blocks(rg_wx[0]),
        row(rg_ba[0]), row(rg_bx[0]), row(rg_lambda[0]), hg_lb_logits.astype(_F32),
        row(hg_norm_g[0]), w_proj_a[0].astype(_F32), w_proj_b[0].astype(_F32),
        w_out[0].astype(_F32))
    lower_bound = jax.nn.softmax(hg_lb_logits.astype(_F32), axis=0)[0]
    fast_ok = jnp.min(lower_bound) >= 2.0 ** (-FAST_LOG2_RANGE / (FAST_BLOCK // 2))
    h_mid = lax.cond(fast_ok, make_mixer(True), make_mixer(False), *mixer_args)

    n_rows = nb * seq
    ffn = pl.pallas_call(
        _ffn_kernel,
        name="ffn",
        grid=(n_rows // FFN_ROWS,),
        in_specs=[
            pl.BlockSpec((FFN_ROWS, d), lambda i: (i, 0)),
            _small((1, d)),
            _resident((d, 2 * D_FF)),
            _resident((D_FF, d)),
            _small((1, d)),
        ],
        out_specs=pl.BlockSpec((FFN_ROWS, d), lambda i: (i, 0)),
        out_shape=jax.ShapeDtypeStruct((n_rows, d), _F32),
        compiler_params=pltpu.CompilerParams(
            dimension_semantics=("arbitrary",), vmem_limit_bytes=VMEM_LIMIT_BYTES),
    )
    out = ffn(h_mid.reshape(n_rows, d), row(norm2_g[0]), w_ffn_in[0].astype(_F32),
              w_ffn_down[0].astype(_F32), row(norm_f_g))
    return out.reshape(nb, seq, d)
```

```python
import functools
import math

import jax
import jax.numpy as jnp
from jax import lax
from jax.experimental import pallas as pl
from jax.experimental.pallas import tpu as pltpu

D_MODEL = 1024
N_META = 16
RG_WIDTH = 1280
RG_BLOCKS = 16
RG_BLOCK = 80
RG_CONV = 4
RG_C = 8.0
HG_HEADS = 8
HG_DK = 128
HG_WIDTH = 1024
D_FF = 2816
NORM_EPS = 1e-6
LOG2_E = math.log2(math.e)

OFF_AX, OFF_AG, OFF_Q, OFF_F, OFF_I, OFF_G, OFF_GA, OFF_GB = (
    0, 1280, 2560, 3584, 4608, 5632, 6656, 7680)

SUBLANES = 8
LANES = 128
TIME_TILE = 128
CHUNK = 64
FAST_BLOCK = 16
FAST_LOG2_RANGE = 96.0
assert FAST_BLOCK % (2 * SUBLANES) == 0 and CHUNK % FAST_BLOCK == 0
PROJ_COLS = 512
FFN_ROWS = 512
MXU_TILE = 256
FFN_PASS_TILES = 6
VMEM_LIMIT_BYTES = 60000 * 1024

_BF = jnp.bfloat16
_F32 = jnp.float32


def _dot(a, b):
    return jnp.dot(a, b, preferred_element_type=_F32)


def _dot_nt(a, b):
    return lax.dot_general(a, b, (((1,), (1,)), ((), ())), preferred_element_type=_F32)


def _dot_tn(a, b):
    return lax.dot_general(a, b, (((0,), (0,)), ((), ())), preferred_element_type=_F32)


def _rmsnorm(x, g):
    ms = jnp.mean(x * x, axis=-1, keepdims=True)
    return x * lax.rsqrt(ms + NORM_EPS) * g


def _sigmoid(x):
    return 0.5 * jnp.tanh(0.5 * x) + 0.5


def _silu(x):
    hx = 0.5 * x
    return hx * jnp.tanh(hx) + hx


def _sqrt_nonneg(x):
    return jnp.where(x > 0.0, x * lax.rsqrt(x), 0.0)


def _gelu_tanh(x):
    c = math.sqrt(2.0 / math.pi)
    return x * (0.5 * (1.0 + jnp.tanh(c * (x + 0.044715 * (x * x * x)))))


def _level_masks():
    t = lax.broadcasted_iota(jnp.int32, (CHUNK, CHUNK), 0)
    s = lax.broadcasted_iota(jnp.int32, (CHUNK, CHUNK), 1)
    masks = {0: t == s}
    h = 1
    while h < CHUNK:
        sh = int(math.log2(2 * h))
        masks[h] = ((t >> sh) == (s >> sh)) & ((t & h) != 0) & ((s & h) == 0)
        h *= 2
    groups = range(0, CHUNK, SUBLANES)
    out = {h: [jnp.where(m[r:r + SUBLANES], 1.0, 0.0).astype(_F32) for r in groups]
           for h, m in masks.items()}
    in_block = ((t // FAST_BLOCK) == (s // FAST_BLOCK)) & (s <= t)
    out["block"] = [in_block[r:r + SUBLANES] for r in groups]
    return out


def _hgrn_operands(g2, q, k, fast):
    w = g2.shape[-1]
    nv = CHUNK // SUBLANES
    sub = lax.broadcasted_iota(jnp.int32, (SUBLANES, w), 0)
    qs = [q[SUBLANES * j:SUBLANES * (j + 1)] for j in range(nv)]
    ks = [k[SUBLANES * j:SUBLANES * (j + 1)] for j in range(nv)]
    bcs = []
    run = None
    for j in range(nv):
        x = g2[SUBLANES * j:SUBLANES * (j + 1)]
        for s in (1, 2, 4):
            x = x + jnp.where(sub >= s, pltpu.roll(x, s, 0), 0.0)
        if run is not None:
            x = x + run
        bcs.append(x)
        run = x[SUBLANES - 1:SUBLANES]
    b_last = run

    def small_ref_row(h, j):
        x = bcs[j]
        rows = [jnp.broadcast_to(x[r:r + 1], x.shape)
                for r in range(h - 1, SUBLANES, 2 * h)]
        out = rows[-1]
        for idx in range(len(rows) - 2, -1, -1):
            out = jnp.where(sub < 2 * h * (idx + 1), rows[idx], out)
        return out

    levels = []
    h = CHUNK // 2
    while h >= (FAST_BLOCK if fast else 1):
        parts = []
        for j in range(nv):
            if h >= SUBLANES:
                r = (SUBLANES * j) // (2 * h) * (2 * h) + h - 1
                ref = bcs[r // SUBLANES][SUBLANES - 1:SUBLANES]
                if (SUBLANES * j) & h:
                    parts.append(qs[j] * jnp.exp2(bcs[j] - ref))
                else:
                    parts.append(ks[j] * jnp.exp2(ref - bcs[j]))
            else:
                base = jnp.where((sub & h) != 0, qs[j], ks[j])
                parts.append(base * jnp.exp2(-jnp.abs(bcs[j] - small_ref_row(h, j))))
        levels.append((h, jnp.concatenate(parts, axis=0).astype(_BF)))
        h //= 2
    if fast:
        pq, pk = [], []
        groups_per_block = FAST_BLOCK // SUBLANES
        for j in range(nv):
            mid = (j // groups_per_block) * groups_per_block + groups_per_block // 2 - 1
            e = bcs[j] - bcs[mid][SUBLANES - 1:SUBLANES]
            pq.append(qs[j] * jnp.exp2(e))
            pk.append(ks[j] * jnp.exp2(-e))
        q_bf = jnp.concatenate(pq, axis=0).astype(_BF)
        k_bf = jnp.concatenate(pk, axis=0).astype(_BF)
    else:
        q_bf = q.astype(_BF)
        k_bf = k.astype(_BF)
    q_dec = jnp.concatenate([qs[j] * jnp.exp2(bcs[j]) for j in range(nv)],
                            axis=0).astype(_BF)
    k_dec = jnp.concatenate([ks[j] * jnp.exp2(b_last - bcs[j]) for j in range(nv)],
                            axis=0).astype(_BF)
    s_dec = jnp.exp2(b_last)
    return levels, q_bf, k_bf, q_dec, k_dec, s_dec


def _hgrn_score_products(operands, masks, tick, fast):
    levels, q_bf, k_bf, q_dec, k_dec, s_dec = operands
    nv = CHUNK // SUBLANES
    all_scores = []
    for hd in range(HG_HEADS):
        sl = slice(hd * HG_DK, (hd + 1) * HG_DK)
        diag = _dot_nt(q_bf[:, sl], k_bf[:, sl])
        if fast:
            acc = [jnp.where(masks["block"][j], diag[SUBLANES * j:SUBLANES * (j + 1)], 0.0)
                   for j in range(nv)]
        else:
            acc = [diag[SUBLANES * j:SUBLANES * (j + 1)] * masks[0][j] for j in range(nv)]
        for h, x in levels:
            prod = _dot_nt(x[:, sl], x[:, sl])
            for j in range(nv):
                if h >= SUBLANES and not (SUBLANES * j) & h:
                    continue
                acc[j] = acc[j] + prod[SUBLANES * j:SUBLANES * (j + 1)] * masks[h][j]
        all_scores.append(jnp.concatenate(acc, axis=0).astype(_BF))
        if hd % 4 == 3:
            tick()
    return all_scores, q_dec, k_dec, s_dec


def _hgrn_apply(phase1, v_bf, st):
    all_scores, q_dec, k_dec, s_dec = phase1
    outs, new_st = [], []
    for hd in range(HG_HEADS):
        sl = slice(hd * HG_DK, (hd + 1) * HG_DK)
        o = _dot(all_scores[hd], v_bf[:, sl])
        o = o + _dot_nt(q_dec[:, sl], st[hd].astype(_BF))
        new_st.append(s_dec[:, sl] * st[hd] + _dot_tn(v_bf[:, sl], k_dec[:, sl]))
        outs.append(o)
    return jnp.concatenate(outs, axis=-1), new_st


def _gate_windows():
    out, base = [], 0
    for n in range(RG_WIDTH // MXU_TILE):
        first_block = n * MXU_TILE // RG_BLOCK
        last_block = ((n + 1) * MXU_TILE - 1) // RG_BLOCK
        k0 = first_block * RG_BLOCK // LANES * LANES
        k1 = -(-(last_block + 1) * RG_BLOCK // LANES) * LANES
        out.append((k0, k1 - k0, base))
        base += k1 - k0
    return out


def _projection_chunks():
    plan = (("ax", OFF_AX, RG_WIDTH), ("f", OFF_F, HG_WIDTH), ("q", OFF_Q, HG_WIDTH),
            ("v", OFF_I, HG_WIDTH), ("ag", OFF_AG, RG_WIDTH), ("ga", OFF_GA, D_MODEL),
            ("g", OFF_G, HG_WIDTH), ("gb", OFF_GB, D_MODEL))
    chunks = []
    for name, off, width in plan:
        for c in range(0, width, PROJ_COLS):
            chunks.append((name, off + c, min(PROJ_COLS, width - c)))
    return chunks


def _stream_chunks(shapes):
    plan = []
    for m, (n_rows, n_cols) in enumerate(shapes):
        parts = -(-n_rows // D_MODEL)
        assert n_rows % (parts * 2 * SUBLANES) == 0 and n_cols % PROJ_COLS == 0
        for r in range(0, n_rows, n_rows // parts):
            for c in range(0, n_cols, PROJ_COLS):
                plan.append((m, r, n_rows // parts, c))
    return plan


def _mixer_kernel(meta_ref, x_ref, n1g_ref, w_in_hbm, convw_ref, convb_ref, wa_ref, wx_ref,
                  ba_ref, bx_ref, lam_ref, lbl_ref, hgn_ref, wpa_hbm, wpb_hbm, wout_hbm,
                  o_ref, w_in_ref, wpa_ref, wpb_ref, wout_ref, w_stage, w_sems, wg_buf, cbuf,
                  hcar, st_ref, *, nb, tt, fast):
    i = pl.program_id(0)
    rows = nb * tt
    is_meta = i == 0

    @pl.when(is_meta)
    def _():
        sources = (w_in_hbm, wpa_hbm, wpb_hbm, wout_hbm)
        kept = (w_in_ref, wpa_ref, wpb_ref, wout_ref)
        plan = _stream_chunks([ref.shape for ref in kept])

        def w_copy(n):
            m, r0, n_rows, c0 = plan[n]
            return pltpu.make_async_copy(
                sources[m].at[pl.ds(r0, n_rows), pl.ds(c0, PROJ_COLS)],
                w_stage.at[n % 2, pl.ds(0, n_rows), :], w_sems.at[n % 2])

        w_copy(0).start()
        for n, (m, r0, n_rows, c0) in enumerate(plan):
            if n + 1 < len(plan):
                w_copy(n + 1).start()
            w_copy(n).wait()
            kept[m][r0:r0 + n_rows, c0:c0 + PROJ_COLS] = w_stage[n % 2, 0:n_rows, :].astype(_BF)
        cbuf[...] = jnp.zeros_like(cbuf)
        hcar[...] = jnp.zeros_like(hcar)
        st_ref[...] = jnp.zeros_like(st_ref)
        blocks = range(RG_BLOCKS)
        for m, w_ref in enumerate((wa_ref, wx_ref)):
            for n, (k0, kn, base) in enumerate(_gate_windows()):
                c0 = n * MXU_TILE
                ee = lax.broadcasted_iota(jnp.int32, (RG_BLOCK, MXU_TILE), 0)
                cc = lax.broadcasted_iota(jnp.int32, (RG_BLOCK, MXU_TILE), 1) + c0
                tiling = functools.reduce(
                    jnp.logical_or, [cc == ee + RG_BLOCK * b for b in blocks])
                rr = lax.broadcasted_iota(jnp.int32, (kn, MXU_TILE), 0) + k0
                cw = lax.broadcasted_iota(jnp.int32, (kn, MXU_TILE), 1) + c0
                same_block = functools.reduce(
                    jnp.logical_or,
                    [(rr >= RG_BLOCK * b) & (rr < RG_BLOCK * (b + 1))
                     & (cw >= RG_BLOCK * b) & (cw < RG_BLOCK * (b + 1)) for b in blocks])
                spread = _dot(w_ref[k0:k0 + kn, :].astype(_BF),
                              jnp.where(tiling, 1.0, 0.0).astype(_BF))
                wg_buf[m, base:base + kn, :] = jnp.where(same_block, spread, 0.0).astype(_BF)

    meta = meta_ref[...]
    h_in = jnp.concatenate([jnp.where(is_meta, meta, x_ref[b]) for b in range(nb)], axis=0)
    hn = _rmsnorm(h_in, n1g_ref[...]).astype(_BF)

    chunks = _projection_chunks()
    issued = {}
    cursor = iter(chunks)

    def project_next(n=1):
        for _ in range(n):
            nxt = next(cursor, None)
            if nxt is None:
                return
            name, off, width = nxt
            issued.setdefault(name, []).append(_dot(hn, w_in_ref[:, off:off + width]))

    def projected(name):
        total = sum(1 for c in chunks if c[0] == name)
        while len(issued.get(name, ())) < total:
            project_next()
        return jnp.concatenate(issued[name], axis=-1)

    a_x = projected("ax")

    convw = convw_ref[...]
    xc_parts = []
    for b in range(nb):
        cbuf[b, SUBLANES:, :] = a_x[b * tt:(b + 1) * tt]
        acc = convb_ref[...] + convw[RG_CONV - 1:RG_CONV] * cbuf[b, pl.ds(SUBLANES, tt), :]
        for j in range(RG_CONV - 1):
            off = SUBLANES - (RG_CONV - 1) + j
            acc = acc + convw[j:j + 1] * cbuf[b, pl.ds(off, tt), :]
        xc_parts.append(acc)
        cbuf[b, 0:SUBLANES, :] = cbuf[b, pl.ds(tt, SUBLANES), :]
    xc = jnp.concatenate(xc_parts, axis=0)
    xc_bf = xc.astype(_BF)
    z = -lam_ref[...]
    softplus = jnp.maximum(z, 0.0) + jnp.log1p(jnp.exp(-jnp.abs(z)))
    a_scale = (-0.5 * RG_C * LOG2_E) * softplus
    a_parts, u_parts = [], []
    for n, (k0, kn, base) in enumerate(_gate_windows()):
        cols = slice(n * MXU_TILE, (n + 1) * MXU_TILE)
        band = xc_bf[:, k0:k0 + kn]
        za = _dot(band, wg_buf[0, base:base + kn, :])
        zx = _dot(band, wg_buf[1, base:base + kn, :])
        r_tanh = jnp.tanh(0.5 * (za + ba_ref[:, cols]))
        a_g = jnp.exp2(a_scale[:, cols] * r_tanh + a_scale[:, cols])
        i_g = _sigmoid(zx + bx_ref[:, cols])
        a_parts.append(a_g)
        u_parts.append(_sqrt_nonneg(1.0 - a_g * a_g) * (i_g * xc[:, cols]))
        if n % 2 == 1:
            project_next()
    a = jnp.concatenate(a_parts, axis=-1)
    u = jnp.concatenate(u_parts, axis=-1)
    t_loc = lax.broadcasted_iota(jnp.int32, (rows, 1), 0) % tt
    u = jnp.where(jnp.logical_or(i > 0, t_loc >= tt - N_META), u, 0.0)

    sub = lax.broadcasted_iota(jnp.int32, (SUBLANES, RG_WIDTH), 0)
    h_parts = []
    for b in range(nb):
        carry = hcar[b]
        for j in range(tt // SUBLANES):
            r0 = b * tt + j * SUBLANES
            av = a[r0:r0 + SUBLANES]
            hv = u[r0:r0 + SUBLANES]
            for s in (1, 2, 4):
                keep = sub >= s
                hv = hv + av * jnp.where(keep, pltpu.roll(hv, s, 0), 0.0)
                av = av * jnp.where(keep, pltpu.roll(av, s, 0), 1.0)
            hv = hv + av * carry
            carry = hv[SUBLANES - 1:SUBLANES]
            h_parts.append(hv)
            if j % 8 == 7:
                project_next()
        hcar[b] = carry
    h_rg = jnp.concatenate(h_parts, axis=0)

    l0 = lbl_ref[0:1, :]
    l1 = lbl_ref[1:2, :]
    lmax = jnp.maximum(l0, l1)
    e0 = jnp.exp(l0 - lmax)
    e1 = jnp.exp(l1 - lmax)
    lb = e0 / (e0 + e1)
    f_sig = _sigmoid(projected("f"))
    log2_f = jnp.log(lb + (1.0 - lb) * f_sig) * LOG2_E
    k_in = (1.0 - lb) * (1.0 - f_sig)
    q_in = _silu(projected("q"))
    project_next()
    v_in = projected("v").astype(_BF)

    masks = _level_masks()
    order = [(b, c) for c in range(tt // CHUNK) for b in range(nb)]
    st = {b: [st_ref[b, hd] for hd in range(HG_HEADS)] for b in range(nb)}
    o_chunks = {}

    def rows_of(b, c):
        r0 = b * tt + c * CHUNK
        return slice(r0, r0 + CHUNK)

    def apply_phase(b, c, phase1):
        o_chunks[(b, c)], st[b] = _hgrn_apply(phase1, v_in[rows_of(b, c)], st[b])

    prepared = []
    for b, c in order:
        sl = rows_of(b, c)
        prepared.append(_hgrn_operands(log2_f[sl], q_in[sl], k_in[sl], fast))
        project_next()
    pending = None
    for (b, c), operands in zip(order, prepared):
        phase1 = _hgrn_score_products(operands, masks, project_next, fast)
        if pending is not None:
            apply_phase(*pending)
        pending = (b, c, phase1)
    apply_phase(*pending)
    for b in range(nb):
        for hd in range(HG_HEADS):
            st_ref[b, hd] = st[b][hd]
    o_b = jnp.concatenate([o_chunks[(b, c)] for b in range(nb) for c in range(tt // CHUNK)],
                          axis=0)

    y_a = (h_rg * _gelu_tanh(projected("ag"))).astype(_BF)
    merged = _sigmoid(projected("ga")) * _dot(y_a, wpa_ref[...])
    g_act = _silu(projected("g"))
    hgn = hgn_ref[...]
    yb_parts = []
    for hd in range(HG_HEADS):
        sl = slice(hd * HG_DK, (hd + 1) * HG_DK)
        yb_parts.append((_rmsnorm(o_b[:, sl], hgn) * g_act[:, sl]).astype(_BF))
    y_b = jnp.concatenate(yb_parts, axis=-1)
    merged = merged + _sigmoid(projected("gb")) * _dot(y_b, wpb_ref[...])

    out = h_in + _dot(merged.astype(_BF), wout_ref[...])
    for b in range(nb):
        o_ref[b] = out[b * tt:(b + 1) * tt]


def _ffn_kernel(h_ref, n2g_ref, wfi_hbm, wfd_hbm, nfg_ref, o_ref, wfi_ref, wfd_ref, w_sems):
    assert D_FF % MXU_TILE == 0
    step = FFN_PASS_TILES * MXU_TILE
    passes = [(lo, min(step, D_FF - lo)) for lo in range(0, D_FF, step)]

    def w_copies(p):
        lo, width = passes[p]
        gate, up, down = pl.ds(lo, width), pl.ds(D_FF + lo, width), pl.ds(lo, width)
        return (
            pltpu.make_async_copy(wfi_hbm.at[:, gate], wfi_ref.at[:, gate], w_sems.at[3 * p]),
            pltpu.make_async_copy(wfi_hbm.at[:, up], wfi_ref.at[:, up], w_sems.at[3 * p + 1]),
            pltpu.make_async_copy(wfd_hbm.at[down, :], wfd_ref.at[down, :], w_sems.at[3 * p + 2]),
        )

    def rows(first):
        if first:
            for p in range(len(passes)):
                for copy in w_copies(p):
                    copy.start()
        arrived = lambda p, k: w_copies(p)[k].wait() if first else None
        h = h_ref[...]
        hn = _rmsnorm(h, n2g_ref[...]).astype(_BF)
        acc = h
        for p, (lo, width) in enumerate(passes):
            arrived(p, 0)
            gcol = _dot(hn, wfi_ref[:, lo:lo + width].astype(_BF))
            arrived(p, 1)
            ucol = _dot(hn, wfi_ref[:, D_FF + lo:D_FF + lo + width].astype(_BF))
            act = (_silu(gcol) * ucol).astype(_BF)
            arrived(p, 2)
            acc = acc + _dot(act, wfd_ref[lo:lo + width, :].astype(_BF))
        o_ref[...] = _rmsnorm(acc, nfg_ref[...])

    is_first = pl.program_id(0) == 0
    pl.when(is_first)(functools.partial(rows, True))
    pl.when(jnp.logical_not(is_first))(functools.partial(rows, False))


def _resident(shape):
    nd = len(shape)
    return pl.BlockSpec(shape, lambda *_: (0,) * nd, pipeline_mode=pl.Buffered(1))


def _small(shape):
    nd = len(shape)
    return pl.BlockSpec(shape, lambda *_: (0,) * nd)


def kernel(x, meta_tokens, norm1_g, w_in, conv_w, conv_b, rg_wa, rg_ba, rg_wx, rg_bx,
           rg_lambda, hg_lb_logits, hg_norm_g, w_proj_a, w_proj_b, w_out, norm2_g,
           w_ffn_in, w_ffn_down, norm_f_g):
    nb, seq, d = x.shape
    assert d == D_MODEL and seq % TIME_TILE == 0 and (nb * seq) % FFN_ROWS == 0
    assert w_in.shape[0] == 1, "single-layer block"
    tt = TIME_TILE
    n_steps = 1 + seq // tt

    meta_tile = jnp.zeros((tt, d), x.dtype).at[tt - N_META:].set(meta_tokens.astype(x.dtype))
    w_in_f = w_in[0].astype(_F32)
    assert w_in_f.shape[1] % PROJ_COLS == 0
    stack_blocks = lambda w: w.reshape(RG_WIDTH, RG_BLOCK).astype(_F32)
    row = lambda v: v.reshape(1, -1).astype(_F32)

    x_map = lambda i: (0, jnp.maximum(i - 1, 0), 0)
    make_mixer = lambda fast: pl.pallas_call(
        functools.partial(_mixer_kernel, nb=nb, tt=tt, fast=fast),
        name="mixer_fast" if fast else "mixer",
        grid=(n_steps,),
        in_specs=[
            _resident((tt, d)),
            pl.BlockSpec((nb, tt, d), x_map),
            _small((1, d)),
            pl.BlockSpec(memory_space=pl.ANY),
            _small((RG_CONV, RG_WIDTH)),
            _small((1, RG_WIDTH)),
            _small((RG_WIDTH, RG_BLOCK)),
            _small((RG_WIDTH, RG_BLOCK)),
            _small((1, RG_WIDTH)),
            _small((1, RG_WIDTH)),
            _small((1, RG_WIDTH)),
            _small((2, HG_WIDTH)),
            _small((1, HG_DK)),
            pl.BlockSpec(memory_space=pl.ANY),
            pl.BlockSpec(memory_space=pl.ANY),
            pl.BlockSpec(memory_space=pl.ANY),
        ],
        out_specs=pl.BlockSpec((nb, tt, d), x_map),
        out_shape=jax.ShapeDtypeStruct((nb, seq, d), _F32),
        scratch_shapes=[
            pltpu.VMEM(w_in_f.shape, _BF),
            pltpu.VMEM((RG_WIDTH, d), _BF),
            pltpu.VMEM((HG_WIDTH, d), _BF),
            pltpu.VMEM((d, d), _BF),
            pltpu.VMEM((2, d, PROJ_COLS), _F32),
            pltpu.SemaphoreType.DMA((2,)),
            pltpu.VMEM((2, sum(w[1] for w in _gate_windows()), MXU_TILE), _BF),
            pltpu.VMEM((nb, tt + SUBLANES, RG_WIDTH), _F32),
            pltpu.VMEM((nb, 1, RG_WIDTH), _F32),
            pltpu.VMEM((nb, HG_HEADS, HG_DK, HG_DK), _F32),
        ],
        compiler_params=pltpu.CompilerParams(
            dimension_semantics=("arbitrary",), vmem_limit_bytes=VMEM_LIMIT_BYTES),
    )
    mixer_args = (
        meta_tile, x, row(norm1_g[0]), w_in_f, conv_w[0].astype(_F32), row(conv_b[0]),
        stack_blocks(rg_wa[0]), stack_blocks(rg_wx[0]),
        row(rg_ba[0]), row(rg_bx[0]), row(rg_lambda[0]), hg_lb_logits.astype(_F32),
        row(hg_norm_g[0]), w_proj_a[0].astype(_F32), w_proj_b[0].astype(_F32),
        w_out[0].astype(_F32))
    lower_bound = jax.nn.softmax(hg_lb_logits.astype(_F32), axis=0)[0]
    fast_ok = jnp.min(lower_bound) >= 2.0 ** (-FAST_LOG2_RANGE / (FAST_BLOCK // 2))
    h_mid = lax.cond(fast_ok, make_mixer(True), make_mixer(False), *mixer_args)

    n_rows = nb * seq
    ffn = pl.pallas_call(
        _ffn_kernel,
        name="ffn",
        grid=(n_rows // FFN_ROWS,),
        in_specs=[
            pl.BlockSpec((FFN_ROWS, d), lambda i: (i, 0)),
            _small((1, d)),
            pl.BlockSpec(memory_space=pl.ANY),
            pl.BlockSpec(memory_space=pl.ANY),
            _small((1, d)),
        ],
        out_specs=pl.BlockSpec((FFN_ROWS, d), lambda i: (i, 0)),
        out_shape=jax.ShapeDtypeStruct((n_rows, d), _F32),
        scratch_shapes=[
            pltpu.VMEM((d, 2 * D_FF), _F32),
            pltpu.VMEM((D_FF, d), _F32),
            pltpu.SemaphoreType.DMA((3 * -(-D_FF // (FFN_PASS_TILES * MXU_TILE)),)),
        ],
        compiler_params=pltpu.CompilerParams(
            dimension_semantics=("arbitrary",), vmem_limit_bytes=VMEM_LIMIT_BYTES),
    )
    out = ffn(h_mid.reshape(n_rows, d), row(norm2_g[0]), w_ffn_in[0].astype(_F32),
              w_ffn_down[0].astype(_F32), row(norm_f_g))
    return out.reshape(nb, seq, d)
```

```python
import functools
import math

import jax
import jax.numpy as jnp
from jax import lax
from jax.experimental import pallas as pl
from jax.experimental.pallas import tpu as pltpu

D_MODEL = 1024
N_META = 16
RG_WIDTH = 1280
RG_BLOCKS = 16
RG_BLOCK = 80
RG_GROUPS = 2
RG_GROUP_W = RG_WIDTH // RG_GROUPS
RG_CONV = 4
RG_C = 8.0
HG_HEADS = 8
HG_DK = 128
HG_WIDTH = 1024
D_FF = 2816
NORM_EPS = 1e-6
LOG2_E = math.log2(math.e)

OFF_AX, OFF_AG, OFF_Q, OFF_F, OFF_I, OFF_G, OFF_GA, OFF_GB = (
    0, 1280, 2560, 3584, 4608, 5632, 6656, 7680)

SUBLANES = 8
TIME_TILE = 128
CHUNK = 64
FAST_BLOCK = 16
FAST_LOG2_RANGE = 96.0
assert FAST_BLOCK % (2 * SUBLANES) == 0 and CHUNK % FAST_BLOCK == 0
PROJ_COLS = 512
FFN_ROWS = 512
MXU_TILE = 256
FFN_PASS_TILES = 6
VMEM_LIMIT_BYTES = 60000 * 1024

_BF = jnp.bfloat16
_F32 = jnp.float32


def _dot(a, b):
    return jnp.dot(a, b, preferred_element_type=_F32)


def _dot_nt(a, b):
    return lax.dot_general(a, b, (((1,), (1,)), ((), ())), preferred_element_type=_F32)


def _dot_tn(a, b):
    return lax.dot_general(a, b, (((0,), (0,)), ((), ())), preferred_element_type=_F32)


def _rmsnorm(x, g):
    ms = jnp.mean(x * x, axis=-1, keepdims=True)
    return x * lax.rsqrt(ms + NORM_EPS) * g


def _sigmoid(x):
    return 0.5 * jnp.tanh(0.5 * x) + 0.5


def _silu(x):
    hx = 0.5 * x
    return hx * jnp.tanh(hx) + hx


def _sqrt_nonneg(x):
    return jnp.where(x > 0.0, x * lax.rsqrt(x), 0.0)


def _gelu_tanh(x):
    c = math.sqrt(2.0 / math.pi)
    return x * (0.5 * (1.0 + jnp.tanh(c * (x + 0.044715 * (x * x * x)))))


def _level_masks():
    t = lax.broadcasted_iota(jnp.int32, (CHUNK, CHUNK), 0)
    s = lax.broadcasted_iota(jnp.int32, (CHUNK, CHUNK), 1)
    masks = {0: t == s}
    h = 1
    while h < CHUNK:
        sh = int(math.log2(2 * h))
        masks[h] = ((t >> sh) == (s >> sh)) & ((t & h) != 0) & ((s & h) == 0)
        h *= 2
    groups = range(0, CHUNK, SUBLANES)
    out = {h: [jnp.where(m[r:r + SUBLANES], 1.0, 0.0).astype(_F32) for r in groups]
           for h, m in masks.items()}
    in_block = ((t // FAST_BLOCK) == (s // FAST_BLOCK)) & (s <= t)
    out["block"] = [in_block[r:r + SUBLANES] for r in groups]
    return out


def _hgrn_operands(g2, q, k, fast):
    w = g2.shape[-1]
    nv = CHUNK // SUBLANES
    sub = lax.broadcasted_iota(jnp.int32, (SUBLANES, w), 0)
    qs = [q[SUBLANES * j:SUBLANES * (j + 1)] for j in range(nv)]
    ks = [k[SUBLANES * j:SUBLANES * (j + 1)] for j in range(nv)]
    bcs = []
    run = None
    for j in range(nv):
        x = g2[SUBLANES * j:SUBLANES * (j + 1)]
        for s in (1, 2, 4):
            x = x + jnp.where(sub >= s, pltpu.roll(x, s, 0), 0.0)
        if run is not None:
            x = x + run
        bcs.append(x)
        run = x[SUBLANES - 1:SUBLANES]
    b_last = run

    def small_ref_row(h, j):
        x = bcs[j]
        rows = [jnp.broadcast_to(x[r:r + 1], x.shape)
                for r in range(h - 1, SUBLANES, 2 * h)]
        out = rows[-1]
        for idx in range(len(rows) - 2, -1, -1):
            out = jnp.where(sub < 2 * h * (idx + 1), rows[idx], out)
        return out

    levels = []
    h = CHUNK // 2
    while h >= (FAST_BLOCK if fast else 1):
        parts = []
        for j in range(nv):
            if h >= SUBLANES:
                r = (SUBLANES * j) // (2 * h) * (2 * h) + h - 1
                ref = bcs[r // SUBLANES][SUBLANES - 1:SUBLANES]
                if (SUBLANES * j) & h:
                    parts.append(qs[j] * jnp.exp2(bcs[j] - ref))
                else:
                    parts.append(ks[j] * jnp.exp2(ref - bcs[j]))
            else:
                base = jnp.where((sub & h) != 0, qs[j], ks[j])
                parts.append(base * jnp.exp2(-jnp.abs(bcs[j] - small_ref_row(h, j))))
        levels.append((h, jnp.concatenate(parts, axis=0).astype(_BF)))
        h //= 2
    if fast:
        pq, pk = [], []
        groups_per_block = FAST_BLOCK // SUBLANES
        for j in range(nv):
            mid = (j // groups_per_block) * groups_per_block + groups_per_block // 2 - 1
            e = bcs[j] - bcs[mid][SUBLANES - 1:SUBLANES]
            pq.append(qs[j] * jnp.exp2(e))
            pk.append(ks[j] * jnp.exp2(-e))
        q_bf = jnp.concatenate(pq, axis=0).astype(_BF)
        k_bf = jnp.concatenate(pk, axis=0).astype(_BF)
    else:
        q_bf = q.astype(_BF)
        k_bf = k.astype(_BF)
    q_dec = jnp.concatenate([qs[j] * jnp.exp2(bcs[j]) for j in range(nv)],
                            axis=0).astype(_BF)
    k_dec = jnp.concatenate([ks[j] * jnp.exp2(b_last - bcs[j]) for j in range(nv)],
                            axis=0).astype(_BF)
    s_dec = jnp.exp2(b_last)
    return levels, q_bf, k_bf, q_dec, k_dec, s_dec


def _hgrn_score_products(operands, masks, tick, fast):
    levels, q_bf, k_bf, q_dec, k_dec, s_dec = operands
    nv = CHUNK // SUBLANES
    all_scores = []
    for hd in range(HG_HEADS):
        sl = slice(hd * HG_DK, (hd + 1) * HG_DK)
        diag = _dot_nt(q_bf[:, sl], k_bf[:, sl])
        if fast:
            acc = [jnp.where(masks["block"][j], diag[SUBLANES * j:SUBLANES * (j + 1)], 0.0)
                   for j in range(nv)]
        else:
            acc = [diag[SUBLANES * j:SUBLANES * (j + 1)] * masks[0][j] for j in range(nv)]
        for h, x in levels:
            if h >= 2 * SUBLANES:
                q_groups = [j for j in range(nv) if (SUBLANES * j) & h]
                lhs = jnp.concatenate([x[r:r + h, sl] for r in range(h, CHUNK, 2 * h)], axis=0)
            else:
                q_groups = [j for j in range(nv) if h < SUBLANES or (SUBLANES * j) & h]
                lhs = x[:, sl]
            prod = _dot_nt(lhs, x[:, sl])
            for idx, j in enumerate(q_groups):
                r = SUBLANES * (idx if h >= 2 * SUBLANES else j)
                acc[j] = acc[j] + prod[r:r + SUBLANES] * masks[h][j]
        all_scores.append(jnp.concatenate(acc, axis=0).astype(_BF))
        if hd % 4 == 3:
            tick()
    return all_scores, q_dec, k_dec, s_dec


def _hgrn_apply(phase1, v_bf, st):
    all_scores, q_dec, k_dec, s_dec = phase1
    outs, new_st = [], []
    for hd in range(HG_HEADS):
        sl = slice(hd * HG_DK, (hd + 1) * HG_DK)
        o = _dot(all_scores[hd], v_bf[:, sl])
        o = o + _dot_nt(q_dec[:, sl], st[hd].astype(_BF))
        new_st.append(s_dec[:, sl] * st[hd] + _dot_tn(v_bf[:, sl], k_dec[:, sl]))
        outs.append(o)
    return jnp.concatenate(outs, axis=-1), new_st


def _projection_chunks():
    plan = (("ax", OFF_AX, RG_WIDTH), ("f", OFF_F, HG_WIDTH), ("q", OFF_Q, HG_WIDTH),
            ("v", OFF_I, HG_WIDTH), ("ag", OFF_AG, RG_WIDTH), ("ga", OFF_GA, D_MODEL),
            ("g", OFF_G, HG_WIDTH), ("gb", OFF_GB, D_MODEL))
    chunks = []
    for name, off, width in plan:
        for c in range(0, width, PROJ_COLS):
            chunks.append((name, off + c, min(PROJ_COLS, width - c)))
    return chunks


def _mixer_kernel(meta_ref, x_ref, n1g_ref, w_in_hbm, convw_ref, convb_ref, wa_ref, wx_ref,
                  ba_ref, bx_ref, lam_ref, lbl_ref, hgn_ref, wpa_ref, wpb_ref, wout_ref,
                  o_ref, w_in_ref, w_stage, w_sems, wg_buf, cbuf, hcar, st_ref, *, nb, tt,
                  fast):
    i = pl.program_id(0)
    rows = nb * tt
    is_meta = i == 0

    @pl.when(is_meta)
    def _():
        n_chunks = w_in_ref.shape[1] // PROJ_COLS

        def w_copy(c):
            return pltpu.make_async_copy(w_in_hbm.at[:, pl.ds(c * PROJ_COLS, PROJ_COLS)],
                                         w_stage.at[c % 2], w_sems.at[c % 2])

        w_copy(0).start()
        for c in range(n_chunks):
            if c + 1 < n_chunks:
                w_copy(c + 1).start()
            w_copy(c).wait()
            w_in_ref[:, c * PROJ_COLS:(c + 1) * PROJ_COLS] = w_stage[c % 2].astype(_BF)
        cbuf[...] = jnp.zeros_like(cbuf)
        hcar[...] = jnp.zeros_like(hcar)
        st_ref[...] = jnp.zeros_like(st_ref)
        kk = lax.broadcasted_iota(jnp.int32, (RG_BLOCK, RG_GROUP_W), 0)
        cc = lax.broadcasted_iota(jnp.int32, (RG_BLOCK, RG_GROUP_W), 1)
        tile_cols = functools.reduce(
            jnp.logical_or, [cc == kk + RG_BLOCK * n for n in range(RG_GROUP_W // RG_BLOCK)])
        tiling = jnp.where(tile_cols, 1.0, 0.0).astype(_BF)
        ri = lax.broadcasted_iota(jnp.int32, (RG_GROUP_W, RG_GROUP_W), 0)
        ci = lax.broadcasted_iota(jnp.int32, (RG_GROUP_W, RG_GROUP_W), 1)
        same_block = functools.reduce(
            jnp.logical_or,
            [jnp.logical_and(jnp.logical_and(ri >= RG_BLOCK * n, ri < RG_BLOCK * (n + 1)),
                             jnp.logical_and(ci >= RG_BLOCK * n, ci < RG_BLOCK * (n + 1)))
             for n in range(RG_GROUP_W // RG_BLOCK)])
        for gidx in range(RG_GROUPS):
            for col, w_ref in ((0, wa_ref), (RG_GROUP_W, wx_ref)):
                spread = _dot(w_ref[gidx].astype(_BF), tiling)
                wg_buf[gidx, :, col:col + RG_GROUP_W] = jnp.where(
                    same_block, spread, 0.0).astype(_BF)

    meta = meta_ref[...]
    h_in = jnp.concatenate([jnp.where(is_meta, meta, x_ref[b]) for b in range(nb)], axis=0)
    hn = _rmsnorm(h_in, n1g_ref[...]).astype(_BF)

    chunks = _projection_chunks()
    issued = {}
    cursor = iter(chunks)

    def project_next(n=1):
        for _ in range(n):
            nxt = next(cursor, None)
            if nxt is None:
                return
            name, off, width = nxt
            issued.setdefault(name, []).append(_dot(hn, w_in_ref[:, off:off + width]))

    def projected(name):
        total = sum(1 for c in chunks if c[0] == name)
        while len(issued.get(name, ())) < total:
            project_next()
        return jnp.concatenate(issued[name], axis=-1)

    a_x = projected("ax")

    convw = convw_ref[...]
    xc_parts = []
    for b in range(nb):
        cbuf[b, SUBLANES:, :] = a_x[b * tt:(b + 1) * tt]
        acc = convb_ref[...] + convw[RG_CONV - 1:RG_CONV] * cbuf[b, pl.ds(SUBLANES, tt), :]
        for j in range(RG_CONV - 1):
            off = SUBLANES - (RG_CONV - 1) + j
            acc = acc + convw[j:j + 1] * cbuf[b, pl.ds(off, tt), :]
        xc_parts.append(acc)
        cbuf[b, 0:SUBLANES, :] = cbuf[b, pl.ds(tt, SUBLANES), :]
    xc = jnp.concatenate(xc_parts, axis=0)
    xc_bf = xc.astype(_BF)
    z = -lam_ref[...]
    softplus = jnp.maximum(z, 0.0) + jnp.log1p(jnp.exp(-jnp.abs(z)))
    a_scale = (-0.5 * RG_C * LOG2_E) * softplus
    a_parts, u_parts = [], []
    for gidx in range(RG_GROUPS):
        lo = gidx * RG_GROUP_W
        cols = slice(lo, lo + RG_GROUP_W)
        zz = _dot(xc_bf[:, cols], wg_buf[gidx])
        r_tanh = jnp.tanh(0.5 * (zz[:, :RG_GROUP_W] + ba_ref[:, cols]))
        a_g = jnp.exp2(a_scale[:, cols] * r_tanh + a_scale[:, cols])
        i_g = _sigmoid(zz[:, RG_GROUP_W:] + bx_ref[:, cols])
        a_parts.append(a_g)
        u_parts.append(_sqrt_nonneg(1.0 - a_g * a_g) * (i_g * xc[:, cols]))
        project_next()
    a = jnp.concatenate(a_parts, axis=-1)
    u = jnp.concatenate(u_parts, axis=-1)
    t_loc = lax.broadcasted_iota(jnp.int32, (rows, 1), 0) % tt
    u = jnp.where(jnp.logical_or(i > 0, t_loc >= tt - N_META), u, 0.0)

    sub = lax.broadcasted_iota(jnp.int32, (SUBLANES, RG_WIDTH), 0)
    h_parts = []
    for b in range(nb):
        carry = hcar[b]
        for j in range(tt // SUBLANES):
            r0 = b * tt + j * SUBLANES
            av = a[r0:r0 + SUBLANES]
            hv = u[r0:r0 + SUBLANES]
            for s in (1, 2, 4):
                keep = sub >= s
                hv = hv + av * jnp.where(keep, pltpu.roll(hv, s, 0), 0.0)
                av = av * jnp.where(keep, pltpu.roll(av, s, 0), 1.0)
            hv = hv + av * carry
            carry = hv[SUBLANES - 1:SUBLANES]
            h_parts.append(hv)
            if j % 8 == 7:
                project_next()
        hcar[b] = carry
    h_rg = jnp.concatenate(h_parts, axis=0)

    l0 = lbl_ref[0:1, :]
    l1 = lbl_ref[1:2, :]
    lmax = jnp.maximum(l0, l1)
    e0 = jnp.exp(l0 - lmax)
    e1 = jnp.exp(l1 - lmax)
    lb = e0 / (e0 + e1)
    f_sig = _sigmoid(projected("f"))
    log2_f = jnp.log(lb + (1.0 - lb) * f_sig) * LOG2_E
    k_in = (1.0 - lb) * (1.0 - f_sig)
    q_in = _silu(projected("q"))
    project_next()
    v_in = projected("v").astype(_BF)

    masks = _level_masks()
    order = [(b, c) for c in range(tt // CHUNK) for b in range(nb)]
    st = {b: [st_ref[b, hd] for hd in range(HG_HEADS)] for b in range(nb)}
    o_chunks = {}

    def rows_of(b, c):
        r0 = b * tt + c * CHUNK
        return slice(r0, r0 + CHUNK)

    def apply_phase(b, c, phase1):
        o_chunks[(b, c)], st[b] = _hgrn_apply(phase1, v_in[rows_of(b, c)], st[b])

    prepared = []
    for b, c in order:
        sl = rows_of(b, c)
        prepared.append(_hgrn_operands(log2_f[sl], q_in[sl], k_in[sl], fast))
        project_next()
    pending = None
    for (b, c), operands in zip(order, prepared):
        phase1 = _hgrn_score_products(operands, masks, project_next, fast)
        if pending is not None:
            apply_phase(*pending)
        pending = (b, c, phase1)
    apply_phase(*pending)
    for b in range(nb):
        for hd in range(HG_HEADS):
            st_ref[b, hd] = st[b][hd]
    o_b = jnp.concatenate([o_chunks[(b, c)] for b in range(nb) for c in range(tt // CHUNK)],
                          axis=0)

    y_a = (h_rg * _gelu_tanh(projected("ag"))).astype(_BF)
    merged = _sigmoid(projected("ga")) * _dot(y_a, wpa_ref[...].astype(_BF))
    g_act = _silu(projected("g"))
    hgn = hgn_ref[...]
    yb_parts = []
    for hd in range(HG_HEADS):
        sl = slice(hd * HG_DK, (hd + 1) * HG_DK)
        yb_parts.append((_rmsnorm(o_b[:, sl], hgn) * g_act[:, sl]).astype(_BF))
    y_b = jnp.concatenate(yb_parts, axis=-1)
    merged = merged + _sigmoid(projected("gb")) * _dot(y_b, wpb_ref[...].astype(_BF))

    out = h_in + _dot(merged.astype(_BF), wout_ref[...].astype(_BF))
    for b in range(nb):
        o_ref[b] = out[b * tt:(b + 1) * tt]


def _ffn_kernel(h_ref, n2g_ref, wfi_ref, wfd_ref, nfg_ref, o_ref):
    h = h_ref[...]
    hn = _rmsnorm(h, n2g_ref[...]).astype(_BF)
    acc = h
    assert D_FF % MXU_TILE == 0
    step = FFN_PASS_TILES * MXU_TILE
    for lo in range(0, D_FF, step):
        width = min(step, D_FF - lo)
        gcol = _dot(hn, wfi_ref[:, lo:lo + width].astype(_BF))
        ucol = _dot(hn, wfi_ref[:, D_FF + lo:D_FF + lo + width].astype(_BF))
        act = (_silu(gcol) * ucol).astype(_BF)
        acc = acc + _dot(act, wfd_ref[lo:lo + width, :].astype(_BF))
    o_ref[...] = _rmsnorm(acc, nfg_ref[...])


def _resident(shape):
    nd = len(shape)
    return pl.BlockSpec(shape, lambda *_: (0,) * nd, pipeline_mode=pl.Buffered(1))


def _small(shape):
    nd = len(shape)
    return pl.BlockSpec(shape, lambda *_: (0,) * nd)


def kernel(x, meta_tokens, norm1_g, w_in, conv_w, conv_b, rg_wa, rg_ba, rg_wx, rg_bx,
           rg_lambda, hg_lb_logits, hg_norm_g, w_proj_a, w_proj_b, w_out, norm2_g,
           w_ffn_in, w_ffn_down, norm_f_g):
    nb, seq, d = x.shape
    assert d == D_MODEL and seq % TIME_TILE == 0 and (nb * seq) % FFN_ROWS == 0
    assert w_in.shape[0] == 1, "single-layer block"
    tt = TIME_TILE
    n_steps = 1 + seq // tt

    meta_tile = jnp.zeros((tt, d), x.dtype).at[tt - N_META:].set(meta_tokens.astype(x.dtype))
    w_in_f = w_in[0].astype(_F32)
    assert w_in_f.shape[1] % PROJ_COLS == 0
    stack_blocks = lambda w: w.reshape(RG_GROUPS, RG_GROUP_W, RG_BLOCK).astype(_F32)
    row = lambda v: v.reshape(1, -1).astype(_F32)

    x_map = lambda i: (0, jnp.maximum(i - 1, 0), 0)
    make_mixer = lambda fast: pl.pallas_call(
        functools.partial(_mixer_kernel, nb=nb, tt=tt, fast=fast),
        name="mixer_fast" if fast else "mixer",
        grid=(n_steps,),
        in_specs=[
            _resident((tt, d)),
            pl.BlockSpec((nb, tt, d), x_map),
            _small((1, d)),
            pl.BlockSpec(memory_space=pl.ANY),
            _small((RG_CONV, RG_WIDTH)),
            _small((1, RG_WIDTH)),
            _small((RG_GROUPS, RG_GROUP_W, RG_BLOCK)),
            _small((RG_GROUPS, RG_GROUP_W, RG_BLOCK)),
            _small((1, RG_WIDTH)),
            _small((1, RG_WIDTH)),
            _small((1, RG_WIDTH)),
            _small((2, HG_WIDTH)),
            _small((1, HG_DK)),
            _resident((RG_WIDTH, d)),
            _resident((HG_WIDTH, d)),
            _resident((d, d)),
        ],
        out_specs=pl.BlockSpec((nb, tt, d), x_map),
        out_shape=jax.ShapeDtypeStruct((nb, seq, d), _F32),
        scratch_shapes=[
            pltpu.VMEM(w_in_f.shape, _BF),
            pltpu.VMEM((2, d, PROJ_COLS), _F32),
            pltpu.SemaphoreType.DMA((2,)),
            pltpu.VMEM((RG_GROUPS, RG_GROUP_W, 2 * RG_GROUP_W), _BF),
            pltpu.VMEM((nb, tt + SUBLANES, RG_WIDTH), _F32),
            pltpu.VMEM((nb, 1, RG_WIDTH), _F32),
            pltpu.VMEM((nb, HG_HEADS, HG_DK, HG_DK), _F32),
        ],
        compiler_params=pltpu.CompilerParams(
            dimension_semantics=("arbitrary",), vmem_limit_bytes=VMEM_LIMIT_BYTES),
    )
    mixer_args = (
        meta_tile, x, row(norm1_g[0]), w_in_f, conv_w[0].astype(_F32), row(conv_b[0]),
        stack_blocks(rg_wa[0]), stack_blocks(rg_wx[0]),
        row(rg_ba[0]), row(rg_bx[0]), row(rg_lambda[0]), hg_lb_logits.astype(_F32),
        row(hg_norm_g[0]), w_proj_a[0].astype(_F32), w_proj_b[0].astype(_F32),
        w_out[0].astype(_F32))
    lower_bound = jax.nn.softmax(hg_lb_logits.astype(_F32), axis=0)[0]
    fast_ok = jnp.min(lower_bound) >= 2.0 ** (-FAST_LOG2_RANGE / (FAST_BLOCK // 2))
    h_mid = lax.cond(fast_ok, make_mixer(True), make_mixer(False), *mixer_args)

    n_rows = nb * seq
    ffn = pl.pallas_call(
        _ffn_kernel,
        name="ffn",
        grid=(n_rows // FFN_ROWS,),
        in_specs=[
            pl.BlockSpec((FFN_ROWS, d), lambda i: (i, 0)),
            _small((1, d)),
            _resident((d, 2 * D_FF)),
            _resident((D_FF, d)),
            _small((1, d)),
        ],
        out_specs=pl.BlockSpec((FFN_ROWS, d), lambda i: (i, 0)),
        out_shape=jax.ShapeDtypeStruct((n_rows, d), _F32),
        compiler_params=pltpu.CompilerParams(
            dimension_semantics=("arbitrary",), vmem_limit_bytes=VMEM_LIMIT_BYTES),
    )
    out = ffn(h_mid.reshape(n_rows, d), row(norm2_g[0]), w_ffn_in[0].astype(_F32),
              w_ffn_down[0].astype(_F32), row(norm_f_g))
    return out.reshape(nb, seq, d)
```

```python
import functools
import math

import jax
import jax.numpy as jnp
from jax import lax
from jax.experimental import pallas as pl
from jax.experimental.pallas import tpu as pltpu

D_MODEL = 1024
N_META = 16
RG_WIDTH = 1280
RG_BLOCKS = 16
RG_BLOCK = 80
RG_GROUPS = 2
RG_GROUP_W = RG_WIDTH // RG_GROUPS
RG_CONV = 4
RG_C = 8.0
HG_HEADS = 8
HG_DK = 128
HG_WIDTH = 1024
D_FF = 2816
NORM_EPS = 1e-6
LOG2_E = math.log2(math.e)

OFF_AX, OFF_AG, OFF_Q, OFF_F, OFF_I, OFF_G, OFF_GA, OFF_GB = (
    0, 1280, 2560, 3584, 4608, 5632, 6656, 7680)

SUBLANES = 8
TIME_TILE = 128
CHUNK = 64
FAST_BLOCK = 16
FAST_LOG2_RANGE = 96.0
assert FAST_BLOCK % (2 * SUBLANES) == 0 and CHUNK % FAST_BLOCK == 0
PROJ_COLS = 512
FFN_ROWS = 512
MXU_TILE = 256
FFN_PASS_TILES = 6
VMEM_LIMIT_BYTES = 60000 * 1024

_BF = jnp.bfloat16
_F32 = jnp.float32


def _dot(a, b):
    return jnp.dot(a, b, preferred_element_type=_F32)


def _dot_nt(a, b):
    return lax.dot_general(a, b, (((1,), (1,)), ((), ())), preferred_element_type=_F32)


def _dot_tn(a, b):
    return lax.dot_general(a, b, (((0,), (0,)), ((), ())), preferred_element_type=_F32)


def _rmsnorm(x, g):
    ms = jnp.mean(x * x, axis=-1, keepdims=True)
    return x * lax.rsqrt(ms + NORM_EPS) * g


def _sigmoid(x):
    return 0.5 * jnp.tanh(0.5 * x) + 0.5


def _silu(x):
    hx = 0.5 * x
    return hx * jnp.tanh(hx) + hx


def _sqrt_nonneg(x):
    return jnp.where(x > 0.0, x * lax.rsqrt(x), 0.0)


def _gelu_tanh(x):
    c = math.sqrt(2.0 / math.pi)
    return x * (0.5 * (1.0 + jnp.tanh(c * (x + 0.044715 * (x * x * x)))))


def _level_masks():
    t = lax.broadcasted_iota(jnp.int32, (CHUNK, CHUNK), 0)
    s = lax.broadcasted_iota(jnp.int32, (CHUNK, CHUNK), 1)
    masks = {0: t == s}
    h = 1
    while h < CHUNK:
        sh = int(math.log2(2 * h))
        masks[h] = ((t >> sh) == (s >> sh)) & ((t & h) != 0) & ((s & h) == 0)
        h *= 2
    groups = range(0, CHUNK, SUBLANES)
    out = {h: [jnp.where(m[r:r + SUBLANES], 1.0, 0.0).astype(_F32) for r in groups]
           for h, m in masks.items()}
    in_block = ((t // FAST_BLOCK) == (s // FAST_BLOCK)) & (s <= t)
    out["block"] = [in_block[r:r + SUBLANES] for r in groups]
    return out


def _hgrn_operands(g2, q, k, fast):
    w = g2.shape[-1]
    nv = CHUNK // SUBLANES
    sub = lax.broadcasted_iota(jnp.int32, (SUBLANES, w), 0)
    qs = [q[SUBLANES * j:SUBLANES * (j + 1)] for j in range(nv)]
    ks = [k[SUBLANES * j:SUBLANES * (j + 1)] for j in range(nv)]
    bcs = []
    run = None
    for j in range(nv):
        x = g2[SUBLANES * j:SUBLANES * (j + 1)]
        for s in (1, 2, 4):
            x = x + jnp.where(sub >= s, pltpu.roll(x, s, 0), 0.0)
        if run is not None:
            x = x + run
        bcs.append(x)
        run = x[SUBLANES - 1:SUBLANES]
    b_last = run

    def small_ref_row(h, j):
        x = bcs[j]
        rows = [jnp.broadcast_to(x[r:r + 1], x.shape)
                for r in range(h - 1, SUBLANES, 2 * h)]
        out = rows[-1]
        for idx in range(len(rows) - 2, -1, -1):
            out = jnp.where(sub < 2 * h * (idx + 1), rows[idx], out)
        return out

    pq, pk, mids = [], [], []
    if fast:
        groups_per_block = FAST_BLOCK // SUBLANES
        for j in range(nv):
            mid = (j // groups_per_block) * groups_per_block + groups_per_block // 2 - 1
            mids.append(mid)
            e = bcs[j] - bcs[mid][SUBLANES - 1:SUBLANES]
            pq.append(qs[j] * jnp.exp2(e))
            pk.append(ks[j] * jnp.exp2(-e))
    rescale = {}

    def block_factor(mid, ref_name, ref, sign):
        key = (mid, ref_name, sign)
        if key not in rescale:
            rescale[key] = jnp.exp2(sign * (bcs[mid][SUBLANES - 1:SUBLANES] - ref))
        return rescale[key]

    levels = []
    h = CHUNK // 2
    while h >= (FAST_BLOCK if fast else 1):
        parts = []
        for j in range(nv):
            if h >= SUBLANES:
                r = (SUBLANES * j) // (2 * h) * (2 * h) + h - 1
                ref = bcs[r // SUBLANES][SUBLANES - 1:SUBLANES]
                is_q = bool((SUBLANES * j) & h)
                if fast:
                    base = pq[j] if is_q else pk[j]
                    parts.append(
                        base * block_factor(mids[j], r, ref, 1.0 if is_q else -1.0))
                elif is_q:
                    parts.append(qs[j] * jnp.exp2(bcs[j] - ref))
                else:
                    parts.append(ks[j] * jnp.exp2(ref - bcs[j]))
            else:
                base = jnp.where((sub & h) != 0, qs[j], ks[j])
                parts.append(base * jnp.exp2(-jnp.abs(bcs[j] - small_ref_row(h, j))))
        levels.append((h, jnp.concatenate(parts, axis=0).astype(_BF)))
        h //= 2
    if fast:
        q_bf = jnp.concatenate(pq, axis=0).astype(_BF)
        k_bf = jnp.concatenate(pk, axis=0).astype(_BF)
        zero = jnp.zeros_like(b_last)
        q_dec = jnp.concatenate(
            [pq[j] * block_factor(mids[j], "start", zero, 1.0) for j in range(nv)],
            axis=0).astype(_BF)
        k_dec = jnp.concatenate(
            [pk[j] * block_factor(mids[j], "end", b_last, -1.0) for j in range(nv)],
            axis=0).astype(_BF)
    else:
        q_bf = q.astype(_BF)
        k_bf = k.astype(_BF)
        q_dec = jnp.concatenate([qs[j] * jnp.exp2(bcs[j]) for j in range(nv)],
                                axis=0).astype(_BF)
        k_dec = jnp.concatenate([ks[j] * jnp.exp2(b_last - bcs[j]) for j in range(nv)],
                                axis=0).astype(_BF)
    s_dec = jnp.exp2(b_last)
    return levels, q_bf, k_bf, q_dec, k_dec, s_dec


def _hgrn_score_products(operands, masks, tick, fast):
    levels, q_bf, k_bf, q_dec, k_dec, s_dec = operands
    nv = CHUNK // SUBLANES
    all_scores = []
    for hd in range(HG_HEADS):
        sl = slice(hd * HG_DK, (hd + 1) * HG_DK)
        diag = _dot_nt(q_bf[:, sl], k_bf[:, sl])
        if fast:
            acc = [jnp.where(masks["block"][j], diag[SUBLANES * j:SUBLANES * (j + 1)], 0.0)
                   for j in range(nv)]
        else:
            acc = [diag[SUBLANES * j:SUBLANES * (j + 1)] * masks[0][j] for j in range(nv)]
        for h, x in levels:
            prod = _dot_nt(x[:, sl], x[:, sl])
            for j in range(nv):
                if h >= SUBLANES and not (SUBLANES * j) & h:
                    continue
                acc[j] = acc[j] + prod[SUBLANES * j:SUBLANES * (j + 1)] * masks[h][j]
        all_scores.append(jnp.concatenate(acc, axis=0).astype(_BF))
        if hd % 4 == 3:
            tick()
    return all_scores, q_dec, k_dec, s_dec


def _hgrn_apply(phase1, v_bf, st):
    all_scores, q_dec, k_dec, s_dec = phase1
    outs, new_st = [], []
    for hd in range(HG_HEADS):
        sl = slice(hd * HG_DK, (hd + 1) * HG_DK)
        o = _dot(all_scores[hd], v_bf[:, sl])
        o = o + _dot_nt(q_dec[:, sl], st[hd].astype(_BF))
        new_st.append(s_dec[:, sl] * st[hd] + _dot_tn(v_bf[:, sl], k_dec[:, sl]))
        outs.append(o)
    return jnp.concatenate(outs, axis=-1), new_st


def _projection_chunks():
    plan = (("ax", OFF_AX, RG_WIDTH), ("f", OFF_F, HG_WIDTH), ("q", OFF_Q, HG_WIDTH),
            ("v", OFF_I, HG_WIDTH), ("ag", OFF_AG, RG_WIDTH), ("ga", OFF_GA, D_MODEL),
            ("g", OFF_G, HG_WIDTH), ("gb", OFF_GB, D_MODEL))
    chunks = []
    for name, off, width in plan:
        for c in range(0, width, PROJ_COLS):
            chunks.append((name, off + c, min(PROJ_COLS, width - c)))
    return chunks


def _mixer_kernel(meta_ref, x_ref, n1g_ref, w_in_hbm, convw_ref, convb_ref, wa_ref, wx_ref,
                  ba_ref, bx_ref, lam_ref, lbl_ref, hgn_ref, wpa_ref, wpb_ref, wout_ref,
                  o_ref, w_in_ref, w_stage, w_sems, wg_buf, cbuf, hcar, st_ref, *, nb, tt,
                  fast):
    i = pl.program_id(0)
    rows = nb * tt
    is_meta = i == 0

    @pl.when(is_meta)
    def _():
        n_chunks = w_in_ref.shape[1] // PROJ_COLS

        def w_copy(c):
            return pltpu.make_async_copy(w_in_hbm.at[:, pl.ds(c * PROJ_COLS, PROJ_COLS)],
                                         w_stage.at[c % 2], w_sems.at[c % 2])

        w_copy(0).start()
        for c in range(n_chunks):
            if c + 1 < n_chunks:
                w_copy(c + 1).start()
            w_copy(c).wait()
            w_in_ref[:, c * PROJ_COLS:(c + 1) * PROJ_COLS] = w_stage[c % 2].astype(_BF)
        cbuf[...] = jnp.zeros_like(cbuf)
        hcar[...] = jnp.zeros_like(hcar)
        st_ref[...] = jnp.zeros_like(st_ref)
        kk = lax.broadcasted_iota(jnp.int32, (RG_BLOCK, RG_GROUP_W), 0)
        cc = lax.broadcasted_iota(jnp.int32, (RG_BLOCK, RG_GROUP_W), 1)
        tile_cols = functools.reduce(
            jnp.logical_or, [cc == kk + RG_BLOCK * n for n in range(RG_GROUP_W // RG_BLOCK)])
        tiling = jnp.where(tile_cols, 1.0, 0.0).astype(_BF)
        ri = lax.broadcasted_iota(jnp.int32, (RG_GROUP_W, RG_GROUP_W), 0)
        ci = lax.broadcasted_iota(jnp.int32, (RG_GROUP_W, RG_GROUP_W), 1)
        same_block = functools.reduce(
            jnp.logical_or,
            [jnp.logical_and(jnp.logical_and(ri >= RG_BLOCK * n, ri < RG_BLOCK * (n + 1)),
                             jnp.logical_and(ci >= RG_BLOCK * n, ci < RG_BLOCK * (n + 1)))
             for n in range(RG_GROUP_W // RG_BLOCK)])
        for gidx in range(RG_GROUPS):
            for col, w_ref in ((0, wa_ref), (RG_GROUP_W, wx_ref)):
                spread = _dot(w_ref[gidx].astype(_BF), tiling)
                wg_buf[gidx, :, col:col + RG_GROUP_W] = jnp.where(
                    same_block, spread, 0.0).astype(_BF)

    meta = meta_ref[...]
    h_in = jnp.concatenate([jnp.where(is_meta, meta, x_ref[b]) for b in range(nb)], axis=0)
    hn = _rmsnorm(h_in, n1g_ref[...]).astype(_BF)

    chunks = _projection_chunks()
    issued = {}
    cursor = iter(chunks)

    def project_next(n=1):
        for _ in range(n):
            nxt = next(cursor, None)
            if nxt is None:
                return
            name, off, width = nxt
            issued.setdefault(name, []).append(_dot(hn, w_in_ref[:, off:off + width]))

    def projected(name):
        total = sum(1 for c in chunks if c[0] == name)
        while len(issued.get(name, ())) < total:
            project_next()
        return jnp.concatenate(issued[name], axis=-1)

    a_x = projected("ax")

    convw = convw_ref[...]
    xc_parts = []
    for b in range(nb):
        cbuf[b, SUBLANES:, :] = a_x[b * tt:(b + 1) * tt]
        acc = convb_ref[...] + convw[RG_CONV - 1:RG_CONV] * cbuf[b, pl.ds(SUBLANES, tt), :]
        for j in range(RG_CONV - 1):
            off = SUBLANES - (RG_CONV - 1) + j
            acc = acc + convw[j:j + 1] * cbuf[b, pl.ds(off, tt), :]
        xc_parts.append(acc)
        cbuf[b, 0:SUBLANES, :] = cbuf[b, pl.ds(tt, SUBLANES), :]
    xc = jnp.concatenate(xc_parts, axis=0)
    xc_bf = xc.astype(_BF)
    z = -lam_ref[...]
    softplus = jnp.maximum(z, 0.0) + jnp.log1p(jnp.exp(-jnp.abs(z)))
    a_scale = (-0.5 * RG_C * LOG2_E) * softplus
    a_parts, u_parts = [], []
    for gidx in range(RG_GROUPS):
        lo = gidx * RG_GROUP_W
        cols = slice(lo, lo + RG_GROUP_W)
        zz = _dot(xc_bf[:, cols], wg_buf[gidx])
        r_tanh = jnp.tanh(0.5 * (zz[:, :RG_GROUP_W] + ba_ref[:, cols]))
        a_g = jnp.exp2(a_scale[:, cols] * r_tanh + a_scale[:, cols])
        i_g = _sigmoid(zz[:, RG_GROUP_W:] + bx_ref[:, cols])
        a_parts.append(a_g)
        u_parts.append(_sqrt_nonneg(1.0 - a_g * a_g) * (i_g * xc[:, cols]))
        project_next()
    a = jnp.concatenate(a_parts, axis=-1)
    u = jnp.concatenate(u_parts, axis=-1)
    t_loc = lax.broadcasted_iota(jnp.int32, (rows, 1), 0) % tt
    u = jnp.where(jnp.logical_or(i > 0, t_loc >= tt - N_META), u, 0.0)

    sub = lax.broadcasted_iota(jnp.int32, (SUBLANES, RG_WIDTH), 0)
    h_parts = []
    for b in range(nb):
        carry = hcar[b]
        for j in range(tt // SUBLANES):
            r0 = b * tt + j * SUBLANES
            av = a[r0:r0 + SUBLANES]
            hv = u[r0:r0 + SUBLANES]
            for s in (1, 2, 4):
                keep = sub >= s
                hv = hv + av * jnp.where(keep, pltpu.roll(hv, s, 0), 0.0)
                av = av * jnp.where(keep, pltpu.roll(av, s, 0), 1.0)
            hv = hv + av * carry
            carry = hv[SUBLANES - 1:SUBLANES]
            h_parts.append(hv)
            if j % 8 == 7:
                project_next()
        hcar[b] = carry
    h_rg = jnp.concatenate(h_parts, axis=0)

    l0 = lbl_ref[0:1, :]
    l1 = lbl_ref[1:2, :]
    lmax = jnp.maximum(l0, l1)
    e0 = jnp.exp(l0 - lmax)
    e1 = jnp.exp(l1 - lmax)
    lb = e0 / (e0 + e1)
    f_sig = _sigmoid(projected("f"))
    log2_f = jnp.log(lb + (1.0 - lb) * f_sig) * LOG2_E
    k_in = (1.0 - lb) * (1.0 - f_sig)
    q_in = _silu(projected("q"))
    project_next()
    v_in = projected("v").astype(_BF)

    masks = _level_masks()
    order = [(b, c) for c in range(tt // CHUNK) for b in range(nb)]
    st = {b: [st_ref[b, hd] for hd in range(HG_HEADS)] for b in range(nb)}
    o_chunks = {}

    def rows_of(b, c):
        r0 = b * tt + c * CHUNK
        return slice(r0, r0 + CHUNK)

    def apply_phase(b, c, phase1):
        o_chunks[(b, c)], st[b] = _hgrn_apply(phase1, v_in[rows_of(b, c)], st[b])

    prepared = []
    for b, c in order:
        sl = rows_of(b, c)
        prepared.append(_hgrn_operands(log2_f[sl], q_in[sl], k_in[sl], fast))
        project_next()
    pending = None
    for (b, c), operands in zip(order, prepared):
        phase1 = _hgrn_score_products(operands, masks, project_next, fast)
        if pending is not None:
            apply_phase(*pending)
        pending = (b, c, phase1)
    apply_phase(*pending)
    for b in range(nb):
        for hd in range(HG_HEADS):
            st_ref[b, hd] = st[b][hd]
    o_b = jnp.concatenate([o_chunks[(b, c)] for b in range(nb) for c in range(tt // CHUNK)],
                          axis=0)

    y_a = (h_rg * _gelu_tanh(projected("ag"))).astype(_BF)
    merged = _sigmoid(projected("ga")) * _dot(y_a, wpa_ref[...].astype(_BF))
    g_act = _silu(projected("g"))
    hgn = hgn_ref[...]
    yb_parts = []
    for hd in range(HG_HEADS):
        sl = slice(hd * HG_DK, (hd + 1) * HG_DK)
        yb_parts.append((_rmsnorm(o_b[:, sl], hgn) * g_act[:, sl]).astype(_BF))
    y_b = jnp.concatenate(yb_parts, axis=-1)
    merged = merged + _sigmoid(projected("gb")) * _dot(y_b, wpb_ref[...].astype(_BF))

    out = h_in + _dot(merged.astype(_BF), wout_ref[...].astype(_BF))
    for b in range(nb):
        o_ref[b] = out[b * tt:(b + 1) * tt]


def _ffn_kernel(h_ref, n2g_ref, wfi_ref, wfd_ref, nfg_ref, o_ref):
    h = h_ref[...]
    hn = _rmsnorm(h, n2g_ref[...]).astype(_BF)
    acc = h
    assert D_FF % MXU_TILE == 0
    step = FFN_PASS_TILES * MXU_TILE
    for lo in range(0, D_FF, step):
        width = min(step, D_FF - lo)
        gcol = _dot(hn, wfi_ref[:, lo:lo + width].astype(_BF))
        ucol = _dot(hn, wfi_ref[:, D_FF + lo:D_FF + lo + width].astype(_BF))
        act = (_silu(gcol) * ucol).astype(_BF)
        acc = acc + _dot(act, wfd_ref[lo:lo + width, :].astype(_BF))
    o_ref[...] = _rmsnorm(acc, nfg_ref[...])


def _resident(shape):
    nd = len(shape)
    return pl.BlockSpec(shape, lambda *_: (0,) * nd, pipeline_mode=pl.Buffered(1))


def _small(shape):
    nd = len(shape)
    return pl.BlockSpec(shape, lambda *_: (0,) * nd)


def kernel(x, meta_tokens, norm1_g, w_in, conv_w, conv_b, rg_wa, rg_ba, rg_wx, rg_bx,
           rg_lambda, hg_lb_logits, hg_norm_g, w_proj_a, w_proj_b, w_out, norm2_g,
           w_ffn_in, w_ffn_down, norm_f_g):
    nb, seq, d = x.shape
    assert d == D_MODEL and seq % TIME_TILE == 0 and (nb * seq) % FFN_ROWS == 0
    assert w_in.shape[0] == 1, "single-layer block"
    tt = TIME_TILE
    n_steps = 1 + seq // tt

    meta_tile = jnp.zeros((tt, d), x.dtype).at[tt - N_META:].set(meta_tokens.astype(x.dtype))
    w_in_f = w_in[0].astype(_F32)
    assert w_in_f.shape[1] % PROJ_COLS == 0
    stack_blocks = lambda w: w.reshape(RG_GROUPS, RG_GROUP_W, RG_BLOCK).astype(_F32)
    row = lambda v: v.reshape(1, -1).astype(_F32)

    x_map = lambda i: (0, jnp.maximum(i - 1, 0), 0)
    make_mixer = lambda fast: pl.pallas_call(
        functools.partial(_mixer_kernel, nb=nb, tt=tt, fast=fast),
        name="mixer_fast" if fast else "mixer",
        grid=(n_steps,),
        in_specs=[
            _resident((tt, d)),
            pl.BlockSpec((nb, tt, d), x_map),
            _small((1, d)),
            pl.BlockSpec(memory_space=pl.ANY),
            _small((RG_CONV, RG_WIDTH)),
            _small((1, RG_WIDTH)),
            _small((RG_GROUPS, RG_GROUP_W, RG_BLOCK)),
            _small((RG_GROUPS, RG_GROUP_W, RG_BLOCK)),
            _small((1, RG_WIDTH)),
            _small((1, RG_WIDTH)),
            _small((1, RG_WIDTH)),
            _small((2, HG_WIDTH)),
            _small((1, HG_DK)),
            _resident((RG_WIDTH, d)),
            _resident((HG_WIDTH, d)),
            _resident((d, d)),
        ],
        out_specs=pl.BlockSpec((nb, tt, d), x_map),
        out_shape=jax.ShapeDtypeStruct((nb, seq, d), _F32),
        scratch_shapes=[
            pltpu.VMEM(w_in_f.shape, _BF),
            pltpu.VMEM((2, d, PROJ_COLS), _F32),
            pltpu.SemaphoreType.DMA((2,)),
            pltpu.VMEM((RG_GROUPS, RG_GROUP_W, 2 * RG_GROUP_W), _BF),
            pltpu.VMEM((nb, tt + SUBLANES, RG_WIDTH), _F32),
            pltpu.VMEM((nb, 1, RG_WIDTH), _F32),
            pltpu.VMEM((nb, HG_HEADS, HG_DK, HG_DK), _F32),
        ],
        compiler_params=pltpu.CompilerParams(
            dimension_semantics=("arbitrary",), vmem_limit_bytes=VMEM_LIMIT_BYTES),
    )
    mixer_args = (
        meta_tile, x, row(norm1_g[0]), w_in_f, conv_w[0].astype(_F32), row(conv_b[0]),
        stack_blocks(rg_wa[0]), stack_blocks(rg_wx[0]),
        row(rg_ba[0]), row(rg_bx[0]), row(rg_lambda[0]), hg_lb_logits.astype(_F32),
        row(hg_norm_g[0]), w_proj_a[0].astype(_F32), w_proj_b[0].astype(_F32),
        w_out[0].astype(_F32))
    lower_bound = jax.nn.softmax(hg_lb_logits.astype(_F32), axis=0)[0]
    fast_ok = jnp.min(lower_bound) >= 2.0 ** (-FAST_LOG2_RANGE / (FAST_BLOCK // 2))
    h_mid = lax.cond(fast_ok, make_mixer(True), make_mixer(False), *mixer_args)

    n_rows = nb * seq
    ffn = pl.pallas_call(
        _ffn_kernel,
        name="ffn",
        grid=(n_rows // FFN_ROWS,),
        in_specs=[
            pl.BlockSpec((FFN_ROWS, d), lambda i: (i, 0)),
            _small((1, d)),
            _resident((d, 2 * D_FF)),
            _resident((D_FF, d)),
            _small((1, d)),
        ],
        out_specs=pl.BlockSpec((FFN_ROWS, d), lambda i: (i, 0)),
        out_shape=jax.ShapeDtypeStruct((n_rows, d), _F32),
        compiler_params=pltpu.CompilerParams(
            dimension_semantics=("arbitrary",), vmem_limit_bytes=VMEM_LIMIT_BYTES),
    )
    out = ffn(h_mid.reshape(n_rows, d), row(norm2_g[0]), w_ffn_in[0].astype(_F32),
              w_ffn_down[0].astype(_F32), row(norm_f_g))
    return out.reshape(nb, seq, d)
```

```python
import functools
import math

import jax
import jax.numpy as jnp
from jax import lax
from jax.experimental import pallas as pl
from jax.experimental.pallas import tpu as pltpu

D_MODEL = 1024
N_META = 16
RG_WIDTH = 1280
RG_BLOCKS = 16
RG_BLOCK = 80
RG_GROUPS = 2
RG_GROUP_W = RG_WIDTH // RG_GROUPS
RG_CONV = 4
RG_C = 8.0
HG_HEADS = 8
HG_DK = 128
HG_WIDTH = 1024
D_FF = 2816
NORM_EPS = 1e-6
LOG2_E = math.log2(math.e)

OFF_AX, OFF_AG, OFF_Q, OFF_F, OFF_I, OFF_G, OFF_GA, OFF_GB = (
    0, 1280, 2560, 3584, 4608, 5632, 6656, 7680)

SUBLANES = 8
TIME_TILE = 128
CHUNK = 64
FAST_BLOCK = 16
FAST_LOG2_RANGE = 96.0
assert FAST_BLOCK % (2 * SUBLANES) == 0 and CHUNK % FAST_BLOCK == 0
PROJ_COLS = 512
FFN_ROWS = 512
MXU_TILE = 256
FFN_PASS_TILES = 6
VMEM_LIMIT_BYTES = 60000 * 1024

_BF = jnp.bfloat16
_F32 = jnp.float32


def _dot(a, b):
    return jnp.dot(a, b, preferred_element_type=_F32)


def _dot_nt(a, b):
    return lax.dot_general(a, b, (((1,), (1,)), ((), ())), preferred_element_type=_F32)


def _dot_tn(a, b):
    return lax.dot_general(a, b, (((0,), (0,)), ((), ())), preferred_element_type=_F32)


def _rmsnorm(x, g):
    ms = jnp.mean(x * x, axis=-1, keepdims=True)
    return x * lax.rsqrt(ms + NORM_EPS) * g


def _sigmoid(x):
    return 0.5 * jnp.tanh(0.5 * x) + 0.5


def _silu(x):
    hx = 0.5 * x
    return hx * jnp.tanh(hx) + hx


def _sqrt_nonneg(x):
    return jnp.where(x > 0.0, x * lax.rsqrt(x), 0.0)


def _gelu_tanh(x):
    c = math.sqrt(2.0 / math.pi)
    return x * (0.5 * (1.0 + jnp.tanh(c * (x + 0.044715 * (x * x * x)))))


def _level_masks():
    t = lax.broadcasted_iota(jnp.int32, (CHUNK, CHUNK), 0)
    s = lax.broadcasted_iota(jnp.int32, (CHUNK, CHUNK), 1)
    masks = {0: t == s}
    h = 1
    while h < CHUNK:
        sh = int(math.log2(2 * h))
        masks[h] = ((t >> sh) == (s >> sh)) & ((t & h) != 0) & ((s & h) == 0)
        h *= 2
    groups = range(0, CHUNK, SUBLANES)
    out = {h: [jnp.where(m[r:r + SUBLANES], 1.0, 0.0).astype(_F32) for r in groups]
           for h, m in masks.items()}
    in_block = ((t // FAST_BLOCK) == (s // FAST_BLOCK)) & (s <= t)
    out["block"] = [in_block[r:r + SUBLANES] for r in groups]
    return out


def _hgrn_operands(g2, q, k, fast):
    w = g2.shape[-1]
    nv = CHUNK // SUBLANES
    sub = lax.broadcasted_iota(jnp.int32, (SUBLANES, w), 0)
    qs = [q[SUBLANES * j:SUBLANES * (j + 1)] for j in range(nv)]
    ks = [k[SUBLANES * j:SUBLANES * (j + 1)] for j in range(nv)]
    bcs = []
    run = None
    for j in range(nv):
        x = g2[SUBLANES * j:SUBLANES * (j + 1)]
        for s in (1, 2, 4):
            x = x + jnp.where(sub >= s, pltpu.roll(x, s, 0), 0.0)
        if run is not None:
            x = x + run
        bcs.append(x)
        run = x[SUBLANES - 1:SUBLANES]
    b_last = run

    def small_ref_row(h, j):
        x = bcs[j]
        rows = [jnp.broadcast_to(x[r:r + 1], x.shape)
                for r in range(h - 1, SUBLANES, 2 * h)]
        out = rows[-1]
        for idx in range(len(rows) - 2, -1, -1):
            out = jnp.where(sub < 2 * h * (idx + 1), rows[idx], out)
        return out

    pq, pk, mids = [], [], []
    if fast:
        groups_per_block = FAST_BLOCK // SUBLANES
        for j in range(nv):
            mid = (j // groups_per_block) * groups_per_block + groups_per_block // 2 - 1
            mids.append(mid)
            e = bcs[j] - bcs[mid][SUBLANES - 1:SUBLANES]
            pq.append(qs[j] * jnp.exp2(e))
            pk.append(ks[j] * jnp.exp2(-e))
    rescale = {}

    def block_factor(mid, ref_name, ref, sign):
        key = (mid, ref_name, sign)
        if key not in rescale:
            rescale[key] = jnp.exp2(sign * (bcs[mid][SUBLANES - 1:SUBLANES] - ref))
        return rescale[key]

    levels = []
    h = CHUNK // 2
    while h >= (FAST_BLOCK if fast else 1):
        parts = []
        for j in range(nv):
            if h >= SUBLANES:
                r = (SUBLANES * j) // (2 * h) * (2 * h) + h - 1
                ref = bcs[r // SUBLANES][SUBLANES - 1:SUBLANES]
                is_q = bool((SUBLANES * j) & h)
                if fast:
                    base = pq[j] if is_q else pk[j]
                    parts.append(
                        base * block_factor(mids[j], r, ref, 1.0 if is_q else -1.0))
                elif is_q:
                    parts.append(qs[j] * jnp.exp2(bcs[j] - ref))
                else:
                    parts.append(ks[j] * jnp.exp2(ref - bcs[j]))
            else:
                base = jnp.where((sub & h) != 0, qs[j], ks[j])
                parts.append(base * jnp.exp2(-jnp.abs(bcs[j] - small_ref_row(h, j))))
        levels.append((h, jnp.concatenate(parts, axis=0).astype(_BF)))
        h //= 2
    if fast:
        q_bf = jnp.concatenate(pq, axis=0).astype(_BF)
        k_bf = jnp.concatenate(pk, axis=0).astype(_BF)
        zero = jnp.zeros_like(b_last)
        q_dec = jnp.concatenate(
            [pq[j] * block_factor(mids[j], "start", zero, 1.0) for j in range(nv)],
            axis=0).astype(_BF)
        k_dec = jnp.concatenate(
            [pk[j] * block_factor(mids[j], "end", b_last, -1.0) for j in range(nv)],
            axis=0).astype(_BF)
    else:
        q_bf = q.astype(_BF)
        k_bf = k.astype(_BF)
        q_dec = jnp.concatenate([qs[j] * jnp.exp2(bcs[j]) for j in range(nv)],
                                axis=0).astype(_BF)
        k_dec = jnp.concatenate([ks[j] * jnp.exp2(b_last - bcs[j]) for j in range(nv)],
                                axis=0).astype(_BF)
    s_dec = jnp.exp2(b_last)
    return levels, q_bf, k_bf, q_dec, k_dec, s_dec


def _hgrn_score_products(operands, masks, tick, fast):
    levels, q_bf, k_bf, q_dec, k_dec, s_dec = operands
    nv = CHUNK // SUBLANES
    all_scores = []
    for hd in range(HG_HEADS):
        sl = slice(hd * HG_DK, (hd + 1) * HG_DK)
        diag = _dot_nt(q_bf[:, sl], k_bf[:, sl])
        if fast:
            acc = [jnp.where(masks["block"][j], diag[SUBLANES * j:SUBLANES * (j + 1)], 0.0)
                   for j in range(nv)]
        else:
            acc = [diag[SUBLANES * j:SUBLANES * (j + 1)] * masks[0][j] for j in range(nv)]
        for h, x in levels:
            prod = _dot_nt(x[:, sl], x[:, sl])
            for j in range(nv):
                if h >= SUBLANES and not (SUBLANES * j) & h:
                    continue
                acc[j] = acc[j] + prod[SUBLANES * j:SUBLANES * (j + 1)] * masks[h][j]
        all_scores.append(jnp.concatenate(acc, axis=0).astype(_BF))
        if hd % 4 == 3:
            tick()
    return all_scores, q_dec, k_dec, s_dec


def _hgrn_apply(phase1, v_bf, st):
    all_scores, q_dec, k_dec, s_dec = phase1
    outs, new_st = [], []
    for hd in range(HG_HEADS):
        sl = slice(hd * HG_DK, (hd + 1) * HG_DK)
        o = _dot(all_scores[hd], v_bf[:, sl])
        o = o + _dot_nt(q_dec[:, sl], st[hd].astype(_BF))
        new_st.append(s_dec[:, sl] * st[hd] + _dot_tn(v_bf[:, sl], k_dec[:, sl]))
        outs.append(o)
    return jnp.concatenate(outs, axis=-1), new_st


def _projection_chunks():
    plan = (("ax", OFF_AX, RG_WIDTH), ("f", OFF_F, HG_WIDTH), ("q", OFF_Q, HG_WIDTH),
            ("v", OFF_I, HG_WIDTH), ("ag", OFF_AG, RG_WIDTH), ("ga", OFF_GA, D_MODEL),
            ("g", OFF_G, HG_WIDTH), ("gb", OFF_GB, D_MODEL))
    chunks = []
    for name, off, width in plan:
        for c in range(0, width, PROJ_COLS):
            chunks.append((name, off + c, min(PROJ_COLS, width - c)))
    return chunks


PARAM_ROWS = 16


def _pack_rows(conv_w, conv_b, rg_ba, rg_bx, rg_lambda, norm1_g, lb_logits, hg_norm_g):
    rows = [conv_w.reshape(RG_CONV, RG_WIDTH)] + [
        v.reshape(-1, v.shape[-1]) for v in
        (conv_b, rg_ba, rg_bx, rg_lambda, norm1_g, lb_logits, hg_norm_g)]
    rows = [jnp.pad(r.astype(_F32), ((0, 0), (0, RG_WIDTH - r.shape[-1]))) for r in rows]
    packed = jnp.concatenate(rows, axis=0)
    return jnp.pad(packed, ((0, PARAM_ROWS - packed.shape[0]), (0, 0)))


def _mixer_kernel(meta_ref, x_ref, p_ref, w_in_hbm, wa_ref, wx_ref, wpa_ref, wpb_ref, wout_ref,
                  o_ref, w_in_ref, w_stage, w_sems, wg_buf, cbuf, hcar, st_ref, *, nb, tt,
                  fast):
    convw_ref = p_ref.at[0:RG_CONV]
    convb_ref, ba_ref, bx_ref, lam_ref = (p_ref.at[r:r + 1] for r in range(4, 8))
    n1g_ref = p_ref.at[8:9, 0:D_MODEL]
    lbl_ref = p_ref.at[9:11, 0:HG_WIDTH]
    hgn_ref = p_ref.at[11:12, 0:HG_DK]
    i = pl.program_id(0)
    rows = nb * tt
    is_meta = i == 0

    @pl.when(is_meta)
    def _():
        n_chunks = w_in_ref.shape[1] // PROJ_COLS

        def w_copy(c):
            return pltpu.make_async_copy(w_in_hbm.at[:, pl.ds(c * PROJ_COLS, PROJ_COLS)],
                                         w_stage.at[c % 2], w_sems.at[c % 2])

        w_copy(0).start()
        for c in range(n_chunks):
            if c + 1 < n_chunks:
                w_copy(c + 1).start()
            w_copy(c).wait()
            w_in_ref[:, c * PROJ_COLS:(c + 1) * PROJ_COLS] = w_stage[c % 2].astype(_BF)
        cbuf[...] = jnp.zeros_like(cbuf)
        hcar[...] = jnp.zeros_like(hcar)
        st_ref[...] = jnp.zeros_like(st_ref)
        kk = lax.broadcasted_iota(jnp.int32, (RG_BLOCK, RG_GROUP_W), 0)
        cc = lax.broadcasted_iota(jnp.int32, (RG_BLOCK, RG_GROUP_W), 1)
        tile_cols = functools.reduce(
            jnp.logical_or, [cc == kk + RG_BLOCK * n for n in range(RG_GROUP_W // RG_BLOCK)])
        tiling = jnp.where(tile_cols, 1.0, 0.0).astype(_BF)
        ri = lax.broadcasted_iota(jnp.int32, (RG_GROUP_W, RG_GROUP_W), 0)
        ci = lax.broadcasted_iota(jnp.int32, (RG_GROUP_W, RG_GROUP_W), 1)
        same_block = functools.reduce(
            jnp.logical_or,
            [jnp.logical_and(jnp.logical_and(ri >= RG_BLOCK * n, ri < RG_BLOCK * (n + 1)),
                             jnp.logical_and(ci >= RG_BLOCK * n, ci < RG_BLOCK * (n + 1)))
             for n in range(RG_GROUP_W // RG_BLOCK)])
        for gidx in range(RG_GROUPS):
            for col, w_ref in ((0, wa_ref), (RG_GROUP_W, wx_ref)):
                spread = _dot(w_ref[gidx].astype(_BF), tiling)
                wg_buf[gidx, :, col:col + RG_GROUP_W] = jnp.where(
                    same_block, spread, 0.0).astype(_BF)

    meta = meta_ref[...]
    h_in = jnp.concatenate([jnp.where(is_meta, meta, x_ref[b]) for b in range(nb)], axis=0)
    hn = _rmsnorm(h_in, n1g_ref[...]).astype(_BF)

    chunks = _projection_chunks()
    issued = {}
    cursor = iter(chunks)

    def project_next(n=1):
        for _ in range(n):
            nxt = next(cursor, None)
            if nxt is None:
                return
            name, off, width = nxt
            issued.setdefault(name, []).append(_dot(hn, w_in_ref[:, off:off + width]))

    def projected(name):
        total = sum(1 for c in chunks if c[0] == name)
        while len(issued.get(name, ())) < total:
            project_next()
        return jnp.concatenate(issued[name], axis=-1)

    a_x = projected("ax")

    convw = convw_ref[...]
    xc_parts = []
    for b in range(nb):
        cbuf[b, SUBLANES:, :] = a_x[b * tt:(b + 1) * tt]
        acc = convb_ref[...] + convw[RG_CONV - 1:RG_CONV] * cbuf[b, pl.ds(SUBLANES, tt), :]
        for j in range(RG_CONV - 1):
            off = SUBLANES - (RG_CONV - 1) + j
            acc = acc + convw[j:j + 1] * cbuf[b, pl.ds(off, tt), :]
        xc_parts.append(acc)
        cbuf[b, 0:SUBLANES, :] = cbuf[b, pl.ds(tt, SUBLANES), :]
    xc = jnp.concatenate(xc_parts, axis=0)
    xc_bf = xc.astype(_BF)
    z = -lam_ref[...]
    softplus = jnp.maximum(z, 0.0) + jnp.log1p(jnp.exp(-jnp.abs(z)))
    a_scale = (-0.5 * RG_C * LOG2_E) * softplus
    a_parts, u_parts = [], []
    for gidx in range(RG_GROUPS):
        lo = gidx * RG_GROUP_W
        cols = slice(lo, lo + RG_GROUP_W)
        zz = _dot(xc_bf[:, cols], wg_buf[gidx])
        r_tanh = jnp.tanh(0.5 * (zz[:, :RG_GROUP_W] + ba_ref[:, cols]))
        a_g = jnp.exp2(a_scale[:, cols] * r_tanh + a_scale[:, cols])
        i_g = _sigmoid(zz[:, RG_GROUP_W:] + bx_ref[:, cols])
        a_parts.append(a_g)
        u_parts.append(_sqrt_nonneg(1.0 - a_g * a_g) * (i_g * xc[:, cols]))
        project_next()
    a = jnp.concatenate(a_parts, axis=-1)
    u = jnp.concatenate(u_parts, axis=-1)
    t_loc = lax.broadcasted_iota(jnp.int32, (rows, 1), 0) % tt
    u = jnp.where(jnp.logical_or(i > 0, t_loc >= tt - N_META), u, 0.0)

    sub = lax.broadcasted_iota(jnp.int32, (SUBLANES, RG_WIDTH), 0)
    h_parts = []
    for b in range(nb):
        carry = hcar[b, 0:1, :]
        for j in range(tt // SUBLANES):
            r0 = b * tt + j * SUBLANES
            av = a[r0:r0 + SUBLANES]
            hv = u[r0:r0 + SUBLANES]
            for s in (1, 2, 4):
                keep = sub >= s
                hv = hv + av * jnp.where(keep, pltpu.roll(hv, s, 0), 0.0)
                av = av * jnp.where(keep, pltpu.roll(av, s, 0), 1.0)
            hv = hv + av * carry
            carry = hv[SUBLANES - 1:SUBLANES]
            h_parts.append(hv)
            if j % 8 == 7:
                project_next()
        hcar[b, 0:1, :] = carry
    h_rg = jnp.concatenate(h_parts, axis=0)

    l0 = lbl_ref[0:1, :]
    l1 = lbl_ref[1:2, :]
    lmax = jnp.maximum(l0, l1)
    e0 = jnp.exp(l0 - lmax)
    e1 = jnp.exp(l1 - lmax)
    lb = e0 / (e0 + e1)
    f_sig = _sigmoid(projected("f"))
    log2_f = jnp.log(lb + (1.0 - lb) * f_sig) * LOG2_E
    k_in = (1.0 - lb) * (1.0 - f_sig)
    q_in = _silu(projected("q"))
    project_next()
    v_in = projected("v").astype(_BF)

    masks = _level_masks()
    order = [(b, c) for c in range(tt // CHUNK) for b in range(nb)]
    st = {b: [st_ref[b, hd] for hd in range(HG_HEADS)] for b in range(nb)}
    o_chunks = {}

    def rows_of(b, c):
        r0 = b * tt + c * CHUNK
        return slice(r0, r0 + CHUNK)

    def apply_phase(b, c, phase1):
        o_chunks[(b, c)], st[b] = _hgrn_apply(phase1, v_in[rows_of(b, c)], st[b])

    prepared = []
    for b, c in order:
        sl = rows_of(b, c)
        prepared.append(_hgrn_operands(log2_f[sl], q_in[sl], k_in[sl], fast))
        project_next()
    pending = None
    for (b, c), operands in zip(order, prepared):
        phase1 = _hgrn_score_products(operands, masks, project_next, fast)
        if pending is not None:
            apply_phase(*pending)
        pending = (b, c, phase1)
    apply_phase(*pending)
    for b in range(nb):
        for hd in range(HG_HEADS):
            st_ref[b, hd] = st[b][hd]
    o_b = jnp.concatenate([o_chunks[(b, c)] for b in range(nb) for c in range(tt // CHUNK)],
                          axis=0)

    y_a = (h_rg * _gelu_tanh(projected("ag"))).astype(_BF)
    merged = _sigmoid(projected("ga")) * _dot(y_a, wpa_ref[...].astype(_BF))
    g_act = _silu(projected("g"))
    hgn = hgn_ref[...]
    yb_parts = []
    for hd in range(HG_HEADS):
        sl = slice(hd * HG_DK, (hd + 1) * HG_DK)
        yb_parts.append((_rmsnorm(o_b[:, sl], hgn) * g_act[:, sl]).astype(_BF))
    y_b = jnp.concatenate(yb_parts, axis=-1)
    merged = merged + _sigmoid(projected("gb")) * _dot(y_b, wpb_ref[...].astype(_BF))

    out = h_in + _dot(merged.astype(_BF), wout_ref[...].astype(_BF))
    for b in range(nb):
        o_ref[b] = out[b * tt:(b + 1) * tt]


def _ffn_kernel(h_ref, n2g_ref, wfi_ref, wfd_ref, nfg_ref, o_ref):
    h = h_ref[...]
    hn = _rmsnorm(h, n2g_ref[...]).astype(_BF)
    acc = h
    assert D_FF % MXU_TILE == 0
    step = FFN_PASS_TILES * MXU_TILE
    for lo in range(0, D_FF, step):
        width = min(step, D_FF - lo)
        gcol = _dot(hn, wfi_ref[:, lo:lo + width].astype(_BF))
        ucol = _dot(hn, wfi_ref[:, D_FF + lo:D_FF + lo + width].astype(_BF))
        act = (_silu(gcol) * ucol).astype(_BF)
        acc = acc + _dot(act, wfd_ref[lo:lo + width, :].astype(_BF))
    o_ref[...] = _rmsnorm(acc, nfg_ref[...])


def _resident(shape):
    nd = len(shape)
    return pl.BlockSpec(shape, lambda *_: (0,) * nd, pipeline_mode=pl.Buffered(1))


def _small(shape):
    nd = len(shape)
    return pl.BlockSpec(shape, lambda *_: (0,) * nd)


def kernel(x, meta_tokens, norm1_g, w_in, conv_w, conv_b, rg_wa, rg_ba, rg_wx, rg_bx,
           rg_lambda, hg_lb_logits, hg_norm_g, w_proj_a, w_proj_b, w_out, norm2_g,
           w_ffn_in, w_ffn_down, norm_f_g):
    nb, seq, d = x.shape
    assert d == D_MODEL and seq % TIME_TILE == 0 and (nb * seq) % FFN_ROWS == 0
    assert w_in.shape[0] == 1, "single-layer block"
    tt = TIME_TILE
    n_steps = 1 + seq // tt

    meta_tile = jnp.zeros((tt, d), x.dtype).at[tt - N_META:].set(meta_tokens.astype(x.dtype))
    w_in_f = w_in[0].astype(_F32)
    assert w_in_f.shape[1] % PROJ_COLS == 0
    stack_blocks = lambda w: w.reshape(RG_GROUPS, RG_GROUP_W, RG_BLOCK).astype(_F32)
    row = lambda v: v.reshape(1, -1).astype(_F32)

    x_map = lambda i: (0, jnp.maximum(i - 1, 0), 0)
    make_mixer = lambda fast: pl.pallas_call(
        functools.partial(_mixer_kernel, nb=nb, tt=tt, fast=fast),
        name="mixer_fast" if fast else "mixer",
        grid=(n_steps,),
        in_specs=[
            _resident((tt, d)),
            pl.BlockSpec((nb, tt, d), x_map),
            _small((PARAM_ROWS, RG_WIDTH)),
            pl.BlockSpec(memory_space=pl.ANY),
            _small((RG_GROUPS, RG_GROUP_W, RG_BLOCK)),
            _small((RG_GROUPS, RG_GROUP_W, RG_BLOCK)),
            _resident((RG_WIDTH, d)),
            _resident((HG_WIDTH, d)),
            _resident((d, d)),
        ],
        out_specs=pl.BlockSpec((nb, tt, d), x_map),
        out_shape=jax.ShapeDtypeStruct((nb, seq, d), _F32),
        scratch_shapes=[
            pltpu.VMEM(w_in_f.shape, _BF),
            pltpu.VMEM((2, d, PROJ_COLS), _F32),
            pltpu.SemaphoreType.DMA((2,)),
            pltpu.VMEM((RG_GROUPS, RG_GROUP_W, 2 * RG_GROUP_W), _BF),
            pltpu.VMEM((nb, tt + SUBLANES, RG_WIDTH), _F32),
            pltpu.VMEM((nb, SUBLANES, RG_WIDTH), _F32),
            pltpu.VMEM((nb, HG_HEADS, HG_DK, HG_DK), _F32),
        ],
        compiler_params=pltpu.CompilerParams(
            dimension_semantics=("arbitrary",), vmem_limit_bytes=VMEM_LIMIT_BYTES),
    )
    mixer_args = (
        meta_tile, x,
        _pack_rows(conv_w[0], conv_b[0], rg_ba[0], rg_bx[0], rg_lambda[0], norm1_g[0],
                   hg_lb_logits, hg_norm_g[0]),
        w_in_f, stack_blocks(rg_wa[0]), stack_blocks(rg_wx[0]),
        w_proj_a[0].astype(_F32), w_proj_b[0].astype(_F32), w_out[0].astype(_F32))
    lower_bound = jax.nn.softmax(hg_lb_logits.astype(_F32), axis=0)[0]
    fast_ok = jnp.min(lower_bound) >= 2.0 ** (-FAST_LOG2_RANGE / (FAST_BLOCK // 2))
    h_mid = lax.cond(fast_ok, make_mixer(True), make_mixer(False), *mixer_args)

    n_rows = nb * seq
    ffn = pl.pallas_call(
        _ffn_kernel,
        name="ffn",
        grid=(n_rows // FFN_ROWS,),
        in_specs=[
            pl.BlockSpec((FFN_ROWS, d), lambda i: (i, 0)),
            _small((1, d)),
            _resident((d, 2 * D_FF)),
            _resident((D_FF, d)),
            _small((1, d)),
        ],
        out_specs=pl.BlockSpec((FFN_ROWS, d), lambda i: (i, 0)),
        out_shape=jax.ShapeDtypeStruct((n_rows, d), _F32),
        compiler_params=pltpu.CompilerParams(
            dimension_semantics=("arbitrary",), vmem_limit_bytes=VMEM_LIMIT_BYTES),
    )
    out = ffn(h_mid.reshape(n_rows, d), row(norm2_g[0]), w_ffn_in[0].astype(_F32),
              w_ffn_down[0].astype(_F32), row(norm_f_g))
    return out.reshape(nb, seq, d)
```
